```python
import math
import jax
import jax.numpy as jnp
from jax import lax
import numpy as np

D_MODEL = 1024
BATCH = 8
SEQ = 2048
DEPTH = 2

N_EVEN = (DEPTH + 1) // 2
N_ODD = DEPTH // 2
HEAD_DIM = 64
A_HEADS = 4
A_MAPS = 2 * A_HEADS
A_V_DIM = 2 * HEAD_DIM
B_Q_HEADS = 8
B_KV_HEADS = 2
B_GROUP = B_Q_HEADS // B_KV_HEADS
A_Q_W = A_MAPS * HEAD_DIM
A_V_W = A_HEADS * A_V_DIM
B_Q_W = B_Q_HEADS * HEAD_DIM
B_KV_W = B_KV_HEADS * HEAD_DIM
IN_SPLITS = (A_Q_W, 2 * A_Q_W, 2 * A_Q_W + A_V_W, 2 * A_Q_W + A_V_W + B_Q_W, 2 * A_Q_W + A_V_W + B_Q_W + B_KV_W)
IN_W = 2 * A_Q_W + A_V_W + B_Q_W + 2 * B_KV_W
MIX_W = A_V_W + B_Q_W
Q_BLOCK = 128
GRID_W = 64
ROPE_AXIS_DIM = HEAD_DIM // 2
ROPE_THETA = 10000.0
REL_BUCKETS = 32
REL_MAX_DIST = 128
RWKV_HEAD = 64
RWKV_HEADS = D_MODEL // RWKV_HEAD
DECAY_LORA = 64
ICLR_LORA = 64
GATE_LORA = 128
GN_EPS = 64e-5
PEER_HEADS = 8
N_KEYS = 128
N_EXPERTS = N_KEYS * N_KEYS
PEER_TOPK = 16
D_KEY = 256
D_KEY_HALF = D_KEY // 2
TOKEN_BLOCK = 128
NORM_EPS = 1e-6

kernel_name = 'hybrid_diffattn_gqa_rwkv7_peer_encoder'


def rms_norm(x, g):
    xf = x.astype(jnp.float32)
    y = xf * lax.rsqrt(jnp.mean(xf * xf, axis=-1, keepdims=True) + NORM_EPS)
    return (y * g.astype(jnp.float32)).astype(x.dtype)


def t5_bucket(rel):
    nb = REL_BUCKETS // 2
    max_exact = nb // 2
    ret = jnp.where(rel > 0, nb, 0)
    n = jnp.abs(rel)
    nf = jnp.maximum(n, 1).astype(jnp.float32)
    large = max_exact + (jnp.log(nf / max_exact) / math.log(REL_MAX_DIST / max_exact) * (nb - max_exact)).astype(jnp.int32)
    large = jnp.minimum(large, nb - 1)
    return ret + jnp.where(n < max_exact, n, large)


def axial_angles(seq_len):
    rows = seq_len // GRID_W
    row = jnp.repeat(jnp.arange(rows), GRID_W).astype(jnp.float32)
    col = jnp.tile(jnp.arange(GRID_W), rows).astype(jnp.float32)
    inv = ROPE_THETA ** (-jnp.arange(0, ROPE_AXIS_DIM, 2, dtype=jnp.float32) / ROPE_AXIS_DIM)
    return row[:, None] * inv, col[:, None] * inv


def rot_half(x, ang):
    x1, x2 = jnp.split(x, 2, axis=-1)
    cos = jnp.cos(ang)[:, None, :]
    sin = jnp.sin(ang)[:, None, :]
    return jnp.concatenate([x1 * cos - x2 * sin, x2 * cos + x1 * sin], axis=-1)


def axial_rope(x, ang_row, ang_col):
    xf = x.astype(jnp.float32)
    out = jnp.concatenate([rot_half(xf[..., :ROPE_AXIS_DIM], ang_row), rot_half(xf[..., ROPE_AXIS_DIM:], ang_col)], axis=-1)
    return out.astype(x.dtype)


def hybrid_attention(h, w_in, qk_g, lam_vec, subln_g, w_out, rel_bias, lambda_init):
    B_, S_, _ = h.shape
    nb = S_ // Q_BLOCK
    scale = HEAD_DIM ** -0.5
    qa, ka, va, qb, kb, vb = jnp.split(h @ w_in, IN_SPLITS, axis=-1)
    qa = rms_norm(qa.reshape(B_, S_, A_MAPS, HEAD_DIM), qk_g[0])
    ka = rms_norm(ka.reshape(B_, S_, A_MAPS, HEAD_DIM), qk_g[1])
    va = va.reshape(B_, S_, A_HEADS, A_V_DIM)
    qb = rms_norm(qb.reshape(B_, S_, B_Q_HEADS, HEAD_DIM), qk_g[2])
    kb = rms_norm(kb.reshape(B_, S_, B_KV_HEADS, HEAD_DIM), qk_g[3])
    vb = vb.reshape(B_, S_, B_KV_HEADS, HEAD_DIM)
    ang_r, ang_c = axial_angles(S_)
    qb = axial_rope(qb, ang_r, ang_c)
    kb = axial_rope(kb, ang_r, ang_c)
    ka_t = ka.transpose(0, 2, 1, 3)
    va_t = va.transpose(0, 2, 1, 3)
    kb_t = kb.transpose(0, 2, 1, 3)
    vb_t = vb.transpose(0, 2, 1, 3)
    qa_blk = qa.reshape(B_, nb, Q_BLOCK, A_MAPS, HEAD_DIM).transpose(1, 0, 3, 2, 4)
    qb_blk = qb.reshape(B_, nb, Q_BLOCK, B_KV_HEADS, B_GROUP, HEAD_DIM).transpose(1, 0, 3, 4, 2, 5)
    lv = lam_vec.astype(jnp.float32)
    lam = jnp.exp(jnp.sum(lv[0] * lv[1])) - jnp.exp(jnp.sum(lv[2] * lv[3])) + lambda_init
    key_pos = jnp.arange(S_)

    def block(args):
        qa_i, qb_i, start = args
        qpos = start + jnp.arange(Q_BLOCK)
        bias = rel_bias[t5_bucket(key_pos[None, :] - qpos[:, None])]
        bias = bias.transpose(2, 0, 1).astype(jnp.float32)
        sa = jnp.einsum('bmqd,bmkd->bmqk', qa_i, ka_t).astype(jnp.float32) * scale + bias
        pa = jax.nn.softmax(sa, axis=-1).reshape(B_, A_HEADS, 2, Q_BLOCK, S_)
        diff = (pa[:, :, 0] - lam * pa[:, :, 1]).astype(va_t.dtype)
        oa = jnp.einsum('bhqk,bhkd->bhqd', diff, va_t)
        sb = jnp.einsum('bgrqd,bgkd->bgrqk', qb_i, kb_t).astype(jnp.float32) * scale
        pb = jax.nn.softmax(sb, axis=-1).astype(vb_t.dtype)
        ob = jnp.einsum('bgrqk,bgkd->bgrqd', pb, vb_t)
        return oa, ob

    oa, ob = lax.map(block, (qa_blk, qb_blk, jnp.arange(nb) * Q_BLOCK))
    oa = oa.transpose(1, 0, 3, 2, 4).reshape(B_, S_, A_HEADS, A_V_DIM)
    oa = (rms_norm(oa, subln_g) * (1.0 - lambda_init)).reshape(B_, S_, A_V_W)
    ob = ob.transpose(1, 0, 4, 2, 3, 5).reshape(B_, S_, B_Q_W)
    return jnp.concatenate([oa, ob], axis=-1) @ w_out


def token_shift_centred(x):
    prev = jnp.pad(x, ((0, 0), (1, 0), (0, 0)))[:, :-1]
    nxt = jnp.pad(x, ((0, 0), (0, 1), (0, 0)))[:, 1:]
    return 0.5 * (prev + nxt) - x


def wkv_scan(r, w, k, v, a_vec, b_vec, reverse):
    B_, S_, H_, N_ = r.shape
    xs = tuple(jnp.moveaxis(t, 1, 0) for t in (r, w, k, v, a_vec, b_vec))

    def step(state, inp):
        r_t, w_t, k_t, v_t, a_t, b_t = inp
        sa = jnp.einsum('bhvk,bhk->bhv', state, a_t)
        state = state * w_t[:, :, None, :] + sa[..., None] * b_t[:, :, None, :] + v_t[..., None] * k_t[:, :, None, :]
        return state, jnp.einsum('bhvk,bhk->bhv', state, r_t)

    s0 = jnp.zeros((B_, H_, N_, N_), jnp.float32)
    _, y = lax.scan(step, s0, xs, reverse=reverse)
    return jnp.moveaxis(y, 0, 1)


def rwkv7_bidir(h, mix, w_rkv, w0, w1, w2, a0, a1, a2, g1, g2, k_k, k_a, r_k, ln_g, ln_b, w_o):
    B_, S_, D_ = h.shape
    f32 = jnp.float32
    xx = token_shift_centred(h)
    xr, xw, xk, xv, xa, xg = (h + xx * mix[i] for i in range(6))
    r = xr @ w_rkv[0]
    k = xk @ w_rkv[1]
    v = xv @ w_rkv[2]
    g = jax.nn.sigmoid(xg @ g1) @ g2
    heads = lambda t: t.astype(f32).reshape(B_, S_, RWKV_HEADS, RWKV_HEAD)
    kk = heads(k * k_k)
    kk = kk / jnp.maximum(jnp.sqrt(jnp.sum(kk * kk, axis=-1, keepdims=True)), 1e-12)
    rf = heads(r)
    vf = heads(v)
    rk = r_k.astype(f32)
    ys = []
    bonuses = []
    for z in range(2):
        wlog = -jax.nn.softplus(-(w0[z] + jnp.tanh(xw @ w1[z]) @ w2[z]).astype(f32)) - 0.5
        decay = heads(jnp.exp(-jnp.exp(wlog)))
        a = jax.nn.sigmoid((a0[z] + (xa @ a1[z]) @ a2[z]).astype(f32))
        kz = heads(k.astype(f32) * (1.0 + (a - 1.0) * k_a.astype(f32)))
        az = heads(a)
        ys.append(wkv_scan(rf, decay, kz, vf, -kk, kk * az, reverse=(z == 1)))
        bonuses.append(jnp.sum(rf * kz * rk, axis=-1, keepdims=True) * vf)
    y = ys[0] + ys[1]
    mu = jnp.mean(y, axis=-1, keepdims=True)
    var = jnp.mean(jnp.square(y - mu), axis=-1, keepdims=True)
    yn = ((y - mu) * lax.rsqrt(var + GN_EPS)).reshape(B_, S_, D_) * ln_g.astype(f32) + ln_b.astype(f32)
    out = (yn + (bonuses[0] + bonuses[1]).reshape(B_, S_, D_)) * g.astype(f32)
    return out.astype(h.dtype) @ w_o


def peer(h, w_q, subkeys, u_tab, v_tab):
    B_, S_, D_ = h.shape
    T = B_ * S_
    ht = h.reshape(T, D_)
    q = (ht @ w_q).reshape(T, PEER_HEADS, 2, D_KEY_HALF)
    s = jnp.einsum('thpc,hpnc->thpn', q, subkeys).astype(jnp.float32)
    s_top, i_top = lax.top_k(s, PEER_TOPK)
    cand = s_top[:, :, 0, :, None] + s_top[:, :, 1, None, :]
    cidx = i_top[:, :, 0, :, None] * N_KEYS + i_top[:, :, 1, None, :]
    best, pos = lax.top_k(cand.reshape(T, PEER_HEADS, PEER_TOPK * PEER_TOPK), PEER_TOPK)
    eidx = jnp.take_along_axis(cidx.reshape(T, PEER_HEADS, PEER_TOPK * PEER_TOPK), pos, axis=-1)
    gate = jax.nn.softmax(best, axis=-1).astype(h.dtype)
    nblk = T // TOKEN_BLOCK

    def expert_block(args):
        xb, eb, gb = args
        u = jnp.take(u_tab, eb, axis=0)
        act = jax.nn.gelu(jnp.einsum('thkd,td->thk', u, xb))
        vv = jnp.take(v_tab, eb, axis=0)
        return jnp.einsum('thk,thkd->td', gb * act, vv)

    y = lax.map(expert_block, (ht.reshape(nblk, TOKEN_BLOCK, D_),
                               eidx.reshape(nblk, TOKEN_BLOCK, PEER_HEADS, PEER_TOPK),
                               gate.reshape(nblk, TOKEN_BLOCK, PEER_HEADS, PEER_TOPK)))
    return y.reshape(B_, S_, D_)


def setup_inputs(seed: int = 0) -> dict:
    key = jax.random.key(seed)
    ks = jax.random.split(key, 32)
    D = D_MODEL
    nrm = lambda k, shape, s: jax.random.normal(k, shape, jnp.float32) * s
    return {
        'x': nrm(ks[0], (BATCH, SEQ, D), 1.0),
        'c': nrm(ks[1], (BATCH, D), 1.0),
        'ada_w': nrm(ks[2], (DEPTH, D, 6 * D), 0.5 * D ** -0.5),
        'ada_b': nrm(ks[3], (DEPTH, 6 * D), 0.01),
        'norm_g': 1.0 + nrm(ks[4], (DEPTH, 2, D), 0.02),
        'attn_w_in': nrm(ks[5], (N_EVEN, D, IN_W), D ** -0.5),
        'attn_qk_g': 1.0 + nrm(ks[6], (N_EVEN, 4, HEAD_DIM), 0.02),
        'diff_lambda': nrm(ks[7], (N_EVEN, 4, HEAD_DIM), 0.1),
        'diff_subln_g': 1.0 + nrm(ks[8], (N_EVEN, A_V_DIM), 0.02),
        'attn_w_out': nrm(ks[9], (N_EVEN, MIX_W, D), MIX_W ** -0.5),
        'rel_bias': nrm(ks[10], (REL_BUCKETS, A_MAPS), 0.2),
        'rwkv_mix': jax.random.uniform(ks[11], (N_ODD, 6, D), jnp.float32),
        'rwkv_w_rkv': nrm(ks[12], (N_ODD, 3, D, D), D ** -0.5),
        'rwkv_w0': jax.random.uniform(ks[13], (N_ODD, 2, D), jnp.float32, -4.0, 0.0),
        'rwkv_w1': nrm(ks[14], (N_ODD, 2, D, DECAY_LORA), D ** -0.5),
        'rwkv_w2': nrm(ks[15], (N_ODD, 2, DECAY_LORA, D), 0.5 * DECAY_LORA ** -0.5),
        'rwkv_a0': nrm(ks[16], (N_ODD, 2, D), 0.1),
        'rwkv_a1': nrm(ks[17], (N_ODD, 2, D, ICLR_LORA), D ** -0.5),
        'rwkv_a2': nrm(ks[18], (N_ODD, 2, ICLR_LORA, D), 0.5 * ICLR_LORA ** -0.5),
        'rwkv_g1': nrm(ks[19], (N_ODD, D, GATE_LORA), D ** -0.5),
        'rwkv_g2': nrm(ks[20], (N_ODD, GATE_LORA, D), GATE_LORA ** -0.5),
        'rwkv_k_k': 0.85 + nrm(ks[21], (N_ODD, D), 0.02),
        'rwkv_k_a': 1.0 + nrm(ks[22], (N_ODD, D), 0.02),
        'rwkv_r_k': nrm(ks[23], (N_ODD, RWKV_HEADS, RWKV_HEAD), 0.1),
        'rwkv_ln_g': 1.0 + nrm(ks[24], (N_ODD, D), 0.02),
        'rwkv_ln_b': nrm(ks[25], (N_ODD, D), 0.01),
        'rwkv_w_o': nrm(ks[26], (N_ODD, D, D), D ** -0.5),
        'peer_w_q': nrm(ks[27], (DEPTH, D, PEER_HEADS * D_KEY), D ** -0.5),
        'peer_subkeys': nrm(ks[28], (DEPTH, PEER_HEADS, 2, N_KEYS, D_KEY_HALF), D_KEY_HALF ** -0.5),
        'peer_u': nrm(ks[29], (DEPTH, N_EXPERTS, D), D ** -0.5),
        'peer_v': nrm(ks[30], (DEPTH, N_EXPERTS, D), 0.5),
    }


def reference(x, c, ada_w, ada_b, norm_g, attn_w_in, attn_qk_g, diff_lambda, diff_subln_g, attn_w_out, rel_bias,
              rwkv_mix, rwkv_w_rkv, rwkv_w0, rwkv_w1, rwkv_w2, rwkv_a0, rwkv_a1, rwkv_a2, rwkv_g1, rwkv_g2,
              rwkv_k_k, rwkv_k_a, rwkv_r_k, rwkv_ln_g, rwkv_ln_b, rwkv_w_o,
              peer_w_q, peer_subkeys, peer_u, peer_v):
    cs = jax.nn.silu(c)
    for layer in range(DEPTH):
        mod = cs @ ada_w[layer] + ada_b[layer]
        sh1, sc1, gt1, sh2, sc2, gt2 = jnp.split(mod[:, None, :], 6, axis=-1)
        h = rms_norm(x, norm_g[layer, 0]) * (1.0 + sc1) + sh1
        j = layer // 2
        if layer % 2 == 0:
            lambda_init = 0.8 - 0.6 * math.exp(-0.3 * layer)
            y = hybrid_attention(h, attn_w_in[j], attn_qk_g[j], diff_lambda[j], diff_subln_g[j], attn_w_out[j], rel_bias, lambda_init)
        else:
            y = rwkv7_bidir(h, rwkv_mix[j], rwkv_w_rkv[j], rwkv_w0[j], rwkv_w1[j], rwkv_w2[j], rwkv_a0[j], rwkv_a1[j], rwkv_a2[j],
                            rwkv_g1[j], rwkv_g2[j], rwkv_k_k[j], rwkv_k_a[j], rwkv_r_k[j], rwkv_ln_g[j], rwkv_ln_b[j], rwkv_w_o[j])
        x = x + (gt1 * y).astype(x.dtype)
        h = rms_norm(x, norm_g[layer, 1]) * (1.0 + sc2) + sh2
        x = x + (gt2 * peer(h, peer_w_q[layer], peer_subkeys[layer], peer_u[layer], peer_v[layer])).astype(x.dtype)
    return x
```

```python
import functools
import math

import jax
import jax.numpy as jnp
from jax import lax
from jax.experimental import pallas as pl
from jax.experimental.pallas import tpu as pltpu

F32 = jnp.float32
BF16 = jnp.bfloat16
HIGHEST = lax.Precision.HIGHEST

D_MODEL = 1024
DEPTH = 2
HEAD_DIM = 64
A_HEADS = 4
A_MAPS = 8
B_Q_HEADS = 8
B_KV_HEADS = 2
GRID_W = 64
ROPE_AXIS_DIM = 32
ROPE_THETA = 10000.0
REL_BUCKETS = 32
REL_MAX_DIST = 128
RWKV_HEAD = 64
GN_EPS = 64e-5
PEER_HEADS = 8
N_KEYS = 128
PEER_TOPK = 16
NORM_EPS = 1e-6

LANES = 128
MXU_DIM = 256
VMEM_LIMIT = 56 * 1024 * 1024

NT_DIMS = (((1,), (1,)), ((), ()))


def _cparams(sem):
    return pltpu.CompilerParams(dimension_semantics=sem, vmem_limit_bytes=VMEM_LIMIT)


def _dot(a, b):
    return jnp.dot(a, b, preferred_element_type=F32)


def _dot_nt(a, b):
    return lax.dot_general(a, b, NT_DIMS, preferred_element_type=F32)


def _segsum(v, bd):
    outs = []
    n = v.shape[1]
    for c0 in range(0, n, MXU_DIM):
        w = min(MXU_DIM, n - c0)
        blk = v[:, c0:c0 + w]
        hi = blk.astype(BF16)
        lo = (blk - hi.astype(F32)).astype(BF16)
        b = bd[:w, :w]
        outs.append(_dot(hi, b) + _dot(lo, b))
    return outs[0] if len(outs) == 1 else jnp.concatenate(outs, axis=1)


def _modnorm(x, g, sh, sc):
    ms = jnp.mean(x * x, axis=-1, keepdims=True)
    return (x * lax.rsqrt(ms + NORM_EPS) * g) * (1.0 + sc) + sh


def _ada_kernel(c_ref, w_ref, b_ref, o_ref):
    c = c_ref[...]
    cs = c * (1.0 / (1.0 + jnp.exp(-c)))
    o_ref[...] = jnp.dot(cs, w_ref[...], preferred_element_type=F32, precision=HIGHEST) + b_ref[...]


def _ada(c, ada_w, ada_b):
    B, D = c.shape
    n6 = ada_w.shape[-1]
    tn = 1536
    out = pl.pallas_call(
        _ada_kernel,
        grid=(DEPTH, n6 // tn),
        in_specs=[
            pl.BlockSpec((B, D), lambda l, j: (0, 0)),
            pl.BlockSpec((None, D, tn), lambda l, j: (l, 0, j)),
            pl.BlockSpec((None, 1, tn), lambda l, j: (l, 0, j)),
        ],
        out_specs=pl.BlockSpec((None, B, tn), lambda l, j: (l, 0, j)),
        out_shape=jax.ShapeDtypeStruct((DEPTH, B, n6), F32),
        compiler_params=_cparams(("parallel", "parallel")),
        name="ada_mod",
    )(c, ada_w, ada_b.reshape(DEPTH, 1, n6))
    return out.reshape(DEPTH, B, 6, D)


C_QA, C_KA, C_VA, C_QB, C_KB, C_VB, C_QBS, C_KBS, C_END = 0, 512, 1024, 1536, 2048, 2304, 2560, 3072, 3328


def _inproj_kernel(x_ref, mod_ref, g_ref, w_ref, gain_ref, cos_ref, sin_ref, bd_ref,
                   qa_ref, ka_ref, va_ref, qb_ref, kb_ref, vb_ref):
    h = _modnorm(x_ref[...], g_ref[...], mod_ref[0:1, :], mod_ref[1:2, :])
    y = _dot(h.astype(BF16), w_ref[...])
    bd = bd_ref[...]
    gain = gain_ref[...]

    def rinv(sec):
        return lax.rsqrt(_segsum(sec * sec, bd) * (1.0 / HEAD_DIM) + NORM_EPS)

    yqa = y[:, C_QA:C_KA]
    qa_ref[...] = (yqa * rinv(yqa) * gain[:, C_QA:C_KA]).astype(BF16)
    yka = y[:, C_KA:C_VA]
    ka_ref[...] = (yka * rinv(yka) * gain[:, C_KA:C_VA]).astype(BF16)
    va_ref[...] = y[:, C_VA:C_QB].astype(BF16)
    cos = cos_ref[...]
    sin = sin_ref[...]
    yqb = y[:, C_QB:C_KB]
    qb = (yqb * gain[:, C_QB:C_KB] * cos + y[:, C_QBS:C_KBS] * gain[:, C_QBS:C_KBS] * sin) * rinv(yqb)
    qb_ref[...] = qb.astype(BF16)
    ykb = y[:, C_KB:C_VB]
    kb = (ykb * gain[:, C_KB:C_VB] * cos[:, :256] + y[:, C_KBS:C_END] * gain[:, C_KBS:C_END] * sin[:, :256]) * rinv(ykb)
    kb_ref[...] = kb.astype(BF16)
    vb_ref[...] = y[:, C_VB:C_QBS].astype(BF16)


def _swap16(a):
    sh = a.shape
    a = a.reshape(sh[:-1] + (sh[-1] // 32, 2, 16))
    return a[..., ::-1, :].reshape(sh)


def _inproj(x2d, mod_l, norm_g, w_in, qk_g, B, S):
    T, D = x2d.shape
    tm = 256
    nper = S // tm
    scale = HEAD_DIM ** -0.5
    wqa, wka, wva, wqb, wkb, wvb = (w_in[:, a:b] for a, b in
                                    ((0, 512), (512, 1024), (1024, 1536), (1536, 2048), (2048, 2176), (2176, 2304)))
    dup = lambda w: jnp.repeat(w.reshape(D, B_KV_HEADS, 1, HEAD_DIM), 2, axis=2).reshape(D, 4 * HEAD_DIM)
    wkbd = dup(wkb)
    w_all = jnp.concatenate([wqa, wka, wva, wqb, wkbd, dup(wvb), _swap16(wqb), _swap16(wkbd)], axis=1).astype(BF16)
    gq = jnp.tile(qk_g[2], B_Q_HEADS) * scale
    gk = jnp.tile(qk_g[3], 4)
    gain = jnp.concatenate([jnp.tile(qk_g[0], A_MAPS) * scale, jnp.tile(qk_g[1], A_MAPS), jnp.ones((512,), F32),
                            gq, gk, jnp.ones((256,), F32), _swap16(gq), _swap16(gk)]).reshape(1, C_END)
    pos = jnp.arange(S)
    row = (pos // GRID_W).astype(F32)
    col = (pos % GRID_W).astype(F32)
    inv = ROPE_THETA ** (-jnp.arange(0, ROPE_AXIS_DIM, 2, dtype=F32) / ROPE_AXIS_DIM)
    ang = jnp.concatenate([row[:, None] * inv] * 2 + [col[:, None] * inv] * 2, axis=1)
    sign = jnp.tile(jnp.concatenate([-jnp.ones((16,), F32), jnp.ones((16,), F32)]), 2)
    cos_t = jnp.tile(jnp.cos(ang), (1, B_Q_HEADS))
    sin_t = jnp.tile(jnp.sin(ang) * sign, (1, B_Q_HEADS))
    bd = jnp.kron(jnp.eye(MXU_DIM // HEAD_DIM, dtype=F32), jnp.ones((HEAD_DIM, HEAD_DIM), F32)).astype(BF16)
    row_spec = lambda w: pl.BlockSpec((tm, w), lambda i: (i, 0))
    const = lambda shp: pl.BlockSpec(shp, lambda i: (0,) * len(shp))
    outs = pl.pallas_call(
        _inproj_kernel,
        grid=(T // tm,),
        in_specs=[
            row_spec(D),
            pl.BlockSpec((None, 6, D), lambda i: (i // nper, 0, 0)),
            const((1, D)),
            const((D, C_END)),
            const((1, C_END)),
            pl.BlockSpec((tm, 512), lambda i: (i % nper, 0)),
            pl.BlockSpec((tm, 512), lambda i: (i % nper, 0)),
            const((MXU_DIM, MXU_DIM)),
        ],
        out_specs=[row_spec(512), row_spec(512), row_spec(512), row_spec(512), row_spec(256), row_spec(256)],
        out_shape=[jax.ShapeDtypeStruct((T, w), BF16) for w in (512, 512, 512, 512, 256, 256)],
        compiler_params=_cparams(("parallel",)),
        name="l0_inproj",
    )(x2d, mod_l, norm_g.reshape(1, D), w_all, gain, cos_t, sin_t, bd)
    return outs


def _t5_bucket(rel):
    nb = REL_BUCKETS // 2
    max_exact = nb // 2
    ret = jnp.where(rel > 0, nb, 0)
    n = jnp.abs(rel)
    nf = jnp.maximum(n, 1).astype(F32)
    large = max_exact + (jnp.log(nf / max_exact) / math.log(REL_MAX_DIST / max_exact) * (nb - max_exact)).astype(jnp.int32)
    large = jnp.minimum(large, nb - 1)
    return ret + jnp.where(n < max_exact, n, large)


def _softmax_rows(s):
    m = jnp.max(s, axis=1, keepdims=True)
    e = jnp.exp(s - m)
    l = jnp.sum(e, axis=1, keepdims=True)
    return e * (1.0 / l)


def _attn_a_kernel(lam_ref, q_ref, k_ref, v_ref, bias_ref, sg_ref, o_ref, *, tq, S, lambda_init):
    i = pl.program_id(2)
    lv = lam_ref[...]
    lam = (jnp.exp(jnp.sum(lv[0:1] * lv[1:2], axis=1, keepdims=True))
           - jnp.exp(jnp.sum(lv[2:3] * lv[3:4], axis=1, keepdims=True)) + lambda_init)
    q = q_ref[...]
    k = k_ref[...]
    lane = lax.broadcasted_iota(jnp.int32, q.shape, 1)
    nsl = tq // LANES + 4
    ps = []
    for mi in range(2):
        qm = jnp.where((lane < HEAD_DIM) == (mi == 0), q, jnp.zeros_like(q))
        s = _dot_nt(qm, k)
        tiles = []
        for j in range(S // LANES):
            idx = jnp.clip(j - i * (tq // LANES) + 2, 0, nsl - 1)
            tiles.append(bias_ref[mi, idx])
        ps.append(_softmax_rows(s + jnp.concatenate(tiles, axis=1)))
    diff = (ps[0] - lam * ps[1]).astype(BF16)
    oa = _dot(diff, v_ref[...])
    ms = jnp.mean(oa * oa, axis=1, keepdims=True)
    o_ref[...] = (oa * lax.rsqrt(ms + NORM_EPS) * sg_ref[...] * (1.0 - lambda_init)).astype(BF16)


def _attn_a(qa, ka, va, diff_lambda, subln_g, rel_bias, lambda_init, B, S):
    tq = 256
    nq = S // tq
    nsl = tq // LANES + 4
    d = jnp.arange(nsl) - 2
    rel = d[:, None, None] * LANES + jnp.arange(LANES)[None, None, :] - jnp.arange(tq)[None, :, None]
    rel = jnp.where(d[:, None, None] < -1, -S, jnp.where(d[:, None, None] > tq // LANES, S, rel))
    bias_tab = rel_bias[_t5_bucket(rel)].transpose(3, 0, 1, 2).astype(F32)
    kern = functools.partial(_attn_a_kernel, tq=tq, S=S, lambda_init=lambda_init)
    return pl.pallas_call(
        kern,
        grid=(B, A_HEADS, nq),
        in_specs=[
            pl.BlockSpec((4, HEAD_DIM), lambda b, h, i: (0, 0)),
            pl.BlockSpec((tq, LANES), lambda b, h, i: (b * nq + i, h)),
            pl.BlockSpec((S, LANES), lambda b, h, i: (b, h)),
            pl.BlockSpec((S, LANES), lambda b, h, i: (b, h)),
            pl.BlockSpec((2, nsl, tq, LANES), lambda b, h, i: (h, 0, 0, 0)),
            pl.BlockSpec((1, LANES), lambda b, h, i: (0, 0)),
        ],
        out_specs=pl.BlockSpec((tq, LANES), lambda b, h, i: (b * nq + i, h)),
        out_shape=jax.ShapeDtypeStruct((B * S, A_HEADS * LANES), BF16),
        compiler_params=_cparams(("parallel", "parallel", "arbitrary")),
        name="l0_attn_diff",
    )(diff_lambda, qa, ka, va, bias_tab, subln_g.reshape(1, LANES))


def _attn_b_kernel(q_ref, k_ref, v_ref, o_ref):
    q = q_ref[...]
    k = k_ref[...]
    v = v_ref[...]
    lane = lax.broadcasted_iota(jnp.int32, (q.shape[0], LANES), 1)
    outs = []
    for u in range(2):
        qp = q[:, u * LANES:(u + 1) * LANES]
        pair = []
        for w in range(2):
            qm = jnp.where((lane < HEAD_DIM) == (w == 0), qp, jnp.zeros_like(qp))
            p = _softmax_rows(_dot_nt(qm, k)).astype(BF16)
            pair.append(_dot(p, v))
        outs.append(jnp.where(lane < HEAD_DIM, pair[0], pair[1]))
    o_ref[...] = jnp.concatenate(outs, axis=1).astype(BF16)


def _attn_b(qb, kb, vb, B, S):
    tq = 256
    nq = S // tq
    return pl.pallas_call(
        _attn_b_kernel,
        grid=(B, B_KV_HEADS, nq),
        in_specs=[
            pl.BlockSpec((tq, 2 * LANES), lambda b, g, i: (b * nq + i, g)),
            pl.BlockSpec((S, LANES), lambda b, g, i: (b, g)),
            pl.BlockSpec((S, LANES), lambda b, g, i: (b, g)),
        ],
        out_specs=pl.BlockSpec((tq, 2 * LANES), lambda b, g, i: (b * nq + i, g)),
        out_shape=jax.ShapeDtypeStruct((B * S, B_Q_HEADS * HEAD_DIM), BF16),
        compiler_params=_cparams(("parallel", "parallel", "arbitrary")),
        name="l0_attn_gqa",
    )(qb, kb, vb)


def _post_kernel(x_ref, ya_ref, yb_ref, wa_ref, wb_ref, mod_ref, g_ref, wq_ref, sk_ref, x1_ref, ht_ref, st_ref):
    y = _dot(ya_ref[...], wa_ref[...]) + _dot(yb_ref[...], wb_ref[...])
    x1 = x_ref[...] + mod_ref[2:3, :] * y
    x1_ref[...] = x1
    h2 = _modnorm(x1, g_ref[...], mod_ref[3:4, :], mod_ref[4:5, :])
    ht_ref[...] = h2.T.astype(BF16)
    q = _dot(h2.astype(BF16), wq_ref[...])
    for hp in range(2 * PEER_HEADS):
        qs = q[:, hp * N_KEYS:(hp + 1) * N_KEYS].astype(BF16)
        st_ref[hp] = _dot_nt(sk_ref[hp], qs)


def _post(x2d, y2d, w_o, mod_l, norm_g, w_q, subkeys, B, S):
    T, D = x2d.shape
    tm = 256
    nper = S // tm
    half = y2d.shape[1] // 2
    nq = w_q.shape[1]
    w_o = w_o.astype(BF16)
    sk = subkeys.reshape(2 * PEER_HEADS, N_KEYS, N_KEYS).astype(BF16)
    const = lambda shp: pl.BlockSpec(shp, lambda i: (0,) * len(shp))
    return pl.pallas_call(
        _post_kernel,
        grid=(T // tm,),
        in_specs=[
            pl.BlockSpec((tm, D), lambda i: (i, 0)),
            pl.BlockSpec((tm, half), lambda i: (i, 0)),
            pl.BlockSpec((tm, half), lambda i: (i, 1)),
            const((half, D)),
            const((half, D)),
            pl.BlockSpec((None, 6, D), lambda i: (i // nper, 0, 0)),
            const((1, D)),
            const((D, nq)),
            const((2 * PEER_HEADS, N_KEYS, N_KEYS)),
        ],
        out_specs=[
            pl.BlockSpec((tm, D), lambda i: (i, 0)),
            pl.BlockSpec((D, tm), lambda i: (0, i)),
            pl.BlockSpec((2 * PEER_HEADS, N_KEYS, tm), lambda i: (0, 0, i)),
        ],
        out_shape=[
            jax.ShapeDtypeStruct((T, D), F32),
            jax.ShapeDtypeStruct((D, T), BF16),
            jax.ShapeDtypeStruct((2 * PEER_HEADS, N_KEYS, T), F32),
        ],
        compiler_params=_cparams(("parallel",)),
        name="mix_out_peer_q",
    )(x2d, y2d, y2d, w_o[:half], w_o[half:], mod_l, norm_g.reshape(1, D), w_q.astype(BF16), sk)


_STAIR = [(i, j) for i in range(PEER_TOPK) for j in range(PEER_TOPK) if (i + 1) * (j + 1) <= PEER_TOPK]
_STAIR_ROWS = 56
_NEG = float("-inf")


def _extract_top(s, iota, nsel, sentinel):
    vals = []
    rank = jnp.full(s.shape, float(nsel), F32)
    for r in range(nsel):
        m = jnp.max(s, axis=0, keepdims=True)
        idx = jnp.min(jnp.where(s == m, iota, sentinel), axis=0, keepdims=True)
        hit = iota == idx
        vals.append(m)
        rank = jnp.where(hit, float(r), rank)
        s = jnp.where(hit, _NEG, s)
    return vals, rank


def _topk_kernel(st_ref, r1_ref, g1_ref, n0_ref, f0_ref, *, tt):
    iota = lax.broadcasted_iota(jnp.int32, (N_KEYS, tt), 0).astype(F32)
    iota_c = lax.broadcasted_iota(jnp.int32, (_STAIR_ROWS, tt), 0).astype(F32)
    starts = []
    for p, (i, j) in enumerate(_STAIR):
        if j == 0:
            starts.append(p)
    group = jnp.zeros((_STAIR_ROWS, tt), F32)
    for st in starts[1:]:
        group = group + jnp.where(iota_c >= float(st), 1.0, 0.0)

    def head_body(h, carry):
        s0 = st_ref[2 * h]
        s1 = st_ref[2 * h + 1]
        a, rank0 = _extract_top(s0, iota, PEER_TOPK, float(N_KEYS))
        b, rank1 = _extract_top(s1, iota, PEER_TOPK, float(N_KEYS))
        cand = jnp.full((_STAIR_ROWS, tt), _NEG, F32)
        for p, (i, j) in enumerate(_STAIR):
            cand = jnp.where(iota_c == float(p), a[i] + b[j], cand)
        _, crank = _extract_top(cand, iota_c, PEER_TOPK, float(_STAIR_ROWS))
        sel = crank < float(PEER_TOPK)
        cmax = a[0] + b[0]
        z = jnp.sum(jnp.where(sel, jnp.exp(cand - cmax), 0.0), axis=0, keepdims=True)
        n0 = jnp.zeros((N_KEYS, tt), F32)
        for i in range(PEER_TOPK):
            cnt = jnp.sum(jnp.where(sel & (group == float(i)), 1.0, 0.0), axis=0, keepdims=True)
            n0 = jnp.where(rank0 == float(i), cnt, n0)
        r1_ref[h] = rank1
        g1_ref[h] = jnp.exp(s1 - b[0])
        n0_ref[h] = n0
        f0_ref[h] = jnp.exp(s0 - a[0]) * (1.0 / z)
        return carry

    lax.fori_loop(0, PEER_HEADS, head_body, 0)


def _topk(st):
    T = st.shape[-1]
    tt = 256
    blk = pl.BlockSpec((PEER_HEADS, N_KEYS, tt), lambda i: (0, 0, i))
    return pl.pallas_call(
        functools.partial(_topk_kernel, tt=tt),
        grid=(T // tt,),
        in_specs=[pl.BlockSpec((2 * PEER_HEADS, N_KEYS, tt), lambda i: (0, 0, i))],
        out_specs=[blk, blk, blk, blk],
        out_shape=[jax.ShapeDtypeStruct((PEER_HEADS, N_KEYS, T), F32)] * 4,
        compiler_params=_cparams(("parallel",)),
        name="peer_topk",
    )(st)


def _peer_dense_kernel(x_ref, mod_ref, ht_ref, u_ref, v_ref, r1_ref, g1_ref, n0_ref, f0_ref, o_ref, acc_ref, *, tb, eb):
    e = pl.program_id(1)

    @pl.when(e == 0)
    def _():
        acc_ref[...] = jnp.zeros_like(acc_ref)

    act = jax.nn.gelu(_dot(u_ref[...], ht_ref[...]))
    zs = []
    for ii in range(eb // N_KEYS):
        w = jnp.zeros((N_KEYS, tb), F32)
        for h in range(PEER_HEADS):
            n0 = n0_ref[h, ii:ii + 1, :]
            f0 = f0_ref[h, ii:ii + 1, :]
            w = w + jnp.where(r1_ref[h] < n0, g1_ref[h] * f0, 0.0)
        zs.append(w * act[ii * N_KEYS:(ii + 1) * N_KEYS])
    z = jnp.concatenate(zs, axis=0)
    acc_ref[...] += _dot(z.T.astype(BF16), v_ref[...])

    @pl.when(e == pl.num_programs(1) - 1)
    def _():
        o_ref[...] = x_ref[...] + mod_ref[5:6, :] * acc_ref[...]


def _peer_dense(x2d, mod_l, ht, u_tab, v_tab, r1, g1, n0, f0, B, S):
    T, D = x2d.shape
    E = u_tab.shape[0]
    tb = min(512, S)
    eb = 1024
    nper = S // tb
    tok = pl.BlockSpec((PEER_HEADS, N_KEYS, tb), lambda t, e: (0, 0, t))
    rows = pl.BlockSpec((PEER_HEADS, eb // N_KEYS, tb), lambda t, e: (0, e, t))
    return pl.pallas_call(
        functools.partial(_peer_dense_kernel, tb=tb, eb=eb),
        grid=(T // tb, E // eb),
        in_specs=[
            pl.BlockSpec((tb, D), lambda t, e: (t, 0)),
            pl.BlockSpec((None, 6, D), lambda t, e: (t // nper, 0, 0)),
            pl.BlockSpec((D, tb), lambda t, e: (0, t)),
            pl.BlockSpec((eb, D), lambda t, e: (e, 0)),
            pl.BlockSpec((eb, D), lambda t, e: (e, 0)),
            tok, tok, rows, rows,
        ],
        out_specs=pl.BlockSpec((tb, D), lambda t, e: (t, 0)),
        out_shape=jax.ShapeDtypeStruct((T, D), F32),
        scratch_shapes=[pltpu.VMEM((tb, D), F32)],
        compiler_params=_cparams(("parallel", "arbitrary")),
        name="peer_dense",
    )(x2d, mod_l, ht, u_tab.astype(BF16), v_tab.astype(BF16), r1, g1, n0, f0)


def _rwkv_pre_kernel(x_ref, xp_ref, xn_ref, mod_ref, g_ref, mix_ref, wr_ref, wk_ref, wv_ref, g1_ref, g2_ref,
                     w1_ref, w2_ref, a1_ref, a2_ref, w0_ref, a0_ref, kk_ref, ka_ref, rk_ref, bd_ref,
                     r_o, v_o, g_o, kn_o, lw_o, kz_o, az_o, bon_o, *, nper):
    i = pl.program_id(0)
    gn = g_ref[...]
    sh = mod_ref[0:1, :]
    sc = mod_ref[1:2, :]
    h = _modnorm(x_ref[...], gn, sh, sc)
    tm = h.shape[0]
    hp = _modnorm(xp_ref[...], gn, sh, sc)[7:8, :]
    hn = _modnorm(xn_ref[...], gn, sh, sc)[0:1, :]
    hp = jnp.where(i % nper == 0, 0.0, hp)
    hn = jnp.where(i % nper == nper - 1, 0.0, hn)
    rows = lax.broadcasted_iota(jnp.int32, h.shape, 0)
    prev = jnp.where(rows == 0, hp, pltpu.roll(h, 1, 0))
    nxt = jnp.where(rows == tm - 1, hn, pltpu.roll(h, tm - 1, 0))
    xx = 0.5 * (prev + nxt) - h
    mix = mix_ref[...]
    xr, xw, xk, xv, xa, xg = ((h + xx * mix[j:j + 1, :]).astype(BF16) for j in range(6))
    r = _dot(xr, wr_ref[...])
    k = _dot(xk, wk_ref[...])
    v = _dot(xv, wv_ref[...])
    gl = _dot(xg, g1_ref[...])
    g = _dot((1.0 / (1.0 + jnp.exp(-gl))).astype(BF16), g2_ref[...])
    wl = _dot(jnp.tanh(_dot(xw, w1_ref[...])).astype(BF16), w2_ref[...])
    al = _dot(_dot(xa, a1_ref[...]).astype(BF16), a2_ref[...])
    bd = bd_ref[...]
    kraw = k * kk_ref[...]
    kn = kraw / jnp.maximum(jnp.sqrt(_segsum(kraw * kraw, bd)), 1e-12)
    r_o[...] = r
    v_o[...] = v
    g_o[...] = g
    kn_o[...] = kn
    D = r.shape[1]
    bon = jnp.zeros_like(r)
    for z in range(2):
        u = -(w0_ref[z:z + 1, :] + wl[:, z * D:(z + 1) * D])
        softplus = jnp.maximum(u, 0.0) + jnp.log(1.0 + jnp.exp(-jnp.abs(u)))
        lw_o[z] = -jnp.exp(-softplus - 0.5)
        a = 1.0 / (1.0 + jnp.exp(-(a0_ref[z:z + 1, :] + al[:, z * D:(z + 1) * D])))
        kz = k * (1.0 + (a - 1.0) * ka_ref[...])
        kz_o[z] = kz
        az_o[z] = a
        bon = bon + _segsum(r * kz * rk_ref[...], bd) * v
    bon_o[...] = bon


def _blockdiag2(w):
    n, D = w.shape[1], w.shape[2]
    z = jnp.zeros((n, D), w.dtype)
    return jnp.concatenate([jnp.concatenate([w[0], z], axis=1), jnp.concatenate([z, w[1]], axis=1)], axis=0)


def _rwkv_pre(x2d, mod_l, norm_g, mix, w_rkv, w0, w1, w2, a0, a1, a2, g1, g2, k_k, k_a, r_k, B, S):
    T, D = x2d.shape
    tm = 256
    nper = S // tm
    nblk8 = T // 8
    bd = jnp.kron(jnp.eye(MXU_DIM // RWKV_HEAD, dtype=F32), jnp.ones((RWKV_HEAD, RWKV_HEAD), F32)).astype(BF16)
    w1c = jnp.concatenate([w1[0], w1[1]], axis=1).astype(BF16)
    a1c = jnp.concatenate([a1[0], a1[1]], axis=1).astype(BF16)
    w2b = _blockdiag2(w2).astype(BF16)
    a2b = _blockdiag2(a2).astype(BF16)
    const = lambda shp: pl.BlockSpec(shp, lambda i: (0,) * len(shp), pipeline_mode=pl.Buffered(1))
    row = pl.BlockSpec((tm, D), lambda i: (i, 0))
    row2 = pl.BlockSpec((2, tm, D), lambda i: (0, i, 0))
    return pl.pallas_call(
        functools.partial(_rwkv_pre_kernel, nper=nper),
        grid=(T // tm,),
        in_specs=[
            row,
            pl.BlockSpec((8, D), lambda i: (jnp.maximum(i * (tm // 8) - 1, 0), 0)),
            pl.BlockSpec((8, D), lambda i: (jnp.minimum((i + 1) * (tm // 8), nblk8 - 1), 0)),
            pl.BlockSpec((None, 6, D), lambda i: (i // nper, 0, 0)),
            const((1, D)), const((6, D)),
            const((D, D)), const((D, D)), const((D, D)),
            const(g1.shape), const(g2.shape),
            const(w1c.shape), const(w2b.shape), const(a1c.shape), const(a2b.shape),
            const((2, D)), const((2, D)), const((1, D)), const((1, D)), const((1, D)),
            const((MXU_DIM, MXU_DIM)),
        ],
        out_specs=[row, row, row, row, row2, row2, row2, row],
        out_shape=[jax.ShapeDtypeStruct((T, D), F32)] * 4 + [jax.ShapeDtypeStruct((2, T, D), F32)] * 3
                  + [jax.ShapeDtypeStruct((T, D), F32)],
        compiler_params=_cparams(("parallel",)),
        name="l1_rwkv_proj",
    )(x2d, x2d, x2d, mod_l, norm_g.reshape(1, D), mix,
      w_rkv[0].astype(BF16), w_rkv[1].astype(BF16), w_rkv[2].astype(BF16), g1.astype(BF16), g2.astype(BF16),
      w1c, w2b, a1c, a2b, w0, a0, k_k.reshape(1, D), k_a.reshape(1, D), r_k.reshape(1, D), bd)


SCAN_C = 64
SCAN_W = 256
SCAN_HEADS = SCAN_W // RWKV_HEAD


def _scan_chunk(r, v, kn, lw, kz, az, S, reverse):
    C = SCAN_C
    n = SCAN_HEADS * C
    ti = lax.broadcasted_iota(jnp.int32, (C, C), 0)
    si = lax.broadcasted_iota(jnp.int32, (C, C), 1)
    inc = ((si >= ti) if reverse else (si <= ti)).astype(F32)
    cum = jnp.dot(inc, lw, preferred_element_type=F32, precision=HIGHEST)
    tot = cum[0:1, :] if reverse else cum[C - 1:C, :]
    a = -kn
    b = kn * az
    pinv = jnp.exp(-cum)
    pend = jnp.exp(tot - cum)
    rt = r * jnp.exp(cum)
    at = a * jnp.exp(cum - lw)
    bt = (b * pinv).astype(BF16)
    kt = (kz * pinv).astype(BF16)

    lane = lax.broadcasted_iota(jnp.int32, (C, SCAN_W), 1) // RWKV_HEAD
    stack = lambda t: jnp.concatenate([jnp.where(lane == hh, t, 0.0) for hh in range(SCAN_HEADS)], axis=0)
    tile = lambda t: jnp.concatenate([t] * SCAN_HEADS, axis=0)
    ri = lax.broadcasted_iota(jnp.int32, (n, n), 0)
    ci = lax.broadcasted_iota(jnp.int32, (n, n), 1)
    same = (ri // C) == (ci // C)
    tt = ri % C
    ss = ci % C
    m_str = same & ((ss > tt) if reverse else (ss < tt))
    m_inc = same & ((ss >= tt) if reverse else (ss <= tt))

    at_s = stack(at).astype(BF16)
    rt_s = stack(rt).astype(BF16)
    bt_t = tile(bt)
    kt_t = tile(kt)
    a_ab = jnp.where(m_str, _dot_nt(at_s, bt_t), 0.0)
    a_ak = jnp.where(m_str, _dot_nt(at_s, kt_t), 0.0)
    a_rb = jnp.where(m_inc, _dot_nt(rt_s, bt_t), 0.0)
    a_rk = jnp.where(m_inc, _dot_nt(rt_s, kt_t), 0.0)

    sb = S.astype(BF16)
    x0 = _dot_nt(jnp.concatenate([at, rt], axis=0).astype(BF16), sb)
    a_s0 = x0[:C]
    r_s0 = x0[C:]
    v_t = tile(v).astype(BF16)
    x = tile(a_s0) + _dot(a_ak.astype(BF16), v_t)
    ap = a_ab
    nsteps = int(math.log2(C))
    for it in range(nsteps):
        apb = ap.astype(BF16)
        x = x + _dot(apb, x.astype(BF16))
        if it + 1 < nsteps:
            ap = _dot(apb, apb)
    y_s = _dot(a_rb.astype(BF16), x.astype(BF16)) + _dot(a_rk.astype(BF16), v_t)
    unstack = lambda t: sum(jnp.where(lane == hh, t[hh * C:(hh + 1) * C], 0.0) for hh in range(SCAN_HEADS))
    u = unstack(x)
    y = r_s0 + unstack(y_s)
    uv = jnp.concatenate([u, v], axis=0)
    bk = jnp.concatenate([b * pend, kz * pend], axis=0)
    upd = _dot(uv.T.astype(BF16), bk.astype(BF16))
    vi = lax.broadcasted_iota(jnp.int32, (SCAN_W, SCAN_W), 0) // RWKV_HEAD
    ki = lax.broadcasted_iota(jnp.int32, (SCAN_W, SCAN_W), 1) // RWKV_HEAD
    s_new = S * jnp.exp(tot) + jnp.where(vi == ki, upd, 0.0)
    return y, s_new


def _scan_kernel(rf, vf, knf, lwf, kzf, azf, rb, vb, knb, lwb, kzb, azb, yf_o, yb_o, s_ref):
    @pl.when(pl.program_id(2) == 0)
    def _():
        s_ref[...] = jnp.zeros_like(s_ref)

    y, s_new = _scan_chunk(rf[...], vf[...], knf[...], lwf[...], kzf[...], azf[...], s_ref[0], False)
    yf_o[...] = y
    s_ref[0] = s_new
    y, s_new = _scan_chunk(rb[...], vb[...], knb[...], lwb[...], kzb[...], azb[...], s_ref[1], True)
    yb_o[...] = y
    s_ref[1] = s_new


def _rwkv_scan(r, v, kn, lw, kz, az, B, S):
    T, D = r.shape
    C = SCAN_C
    nc = S // C
    ng = D // SCAN_W
    fwd = lambda b, g, c: (b * nc + c, g)
    bwd = lambda b, g, c: (b * nc + nc - 1 - c, g)
    fwd3 = lambda z: (lambda b, g, c: (z, b * nc + c, g))
    bwd3 = lambda z: (lambda b, g, c: (z, b * nc + nc - 1 - c, g))
    s2 = lambda im: pl.BlockSpec((C, SCAN_W), im)
    s3 = lambda im: pl.BlockSpec((None, C, SCAN_W), im)
    return pl.pallas_call(
        _scan_kernel,
        grid=(B, ng, nc),
        in_specs=[s2(fwd), s2(fwd), s2(fwd), s3(fwd3(0)), s3(fwd3(0)), s3(fwd3(0)),
                  s2(bwd), s2(bwd), s2(bwd), s3(bwd3(1)), s3(bwd3(1)), s3(bwd3(1))],
        out_specs=[s2(fwd), s2(bwd)],
        out_shape=[jax.ShapeDtypeStruct((T, D), F32)] * 2,
        scratch_shapes=[pltpu.VMEM((2, SCAN_W, SCAN_W), F32)],
        compiler_params=_cparams(("parallel", "parallel", "arbitrary")),
        name="l1_wkv_scan",
    )(r, v, kn, lw, kz, az, r, v, kn, lw, kz, az)


def _rwkv_out_kernel(yf_ref, yb_ref, bon_ref, g_ref, lg_ref, lb_ref, bd_ref, o_ref):
    y = yf_ref[...] + yb_ref[...]
    bd = bd_ref[...]
    mu = _segsum(y, bd) * (1.0 / RWKV_HEAD)
    yc = y - mu
    var = _segsum(yc * yc, bd) * (1.0 / RWKV_HEAD)
    yn = yc * lax.rsqrt(var + GN_EPS) * lg_ref[...] + lb_ref[...]
    o_ref[...] = ((yn + bon_ref[...]) * g_ref[...]).astype(BF16)


def _rwkv_out(yf, yb, bon, g, ln_g, ln_b):
    T, D = yf.shape
    tm = 256
    bd = jnp.kron(jnp.eye(MXU_DIM // RWKV_HEAD, dtype=F32), jnp.ones((RWKV_HEAD, RWKV_HEAD), F32)).astype(BF16)
    row = pl.BlockSpec((tm, D), lambda i: (i, 0))
    const = lambda shp: pl.BlockSpec(shp, lambda i: (0,) * len(shp))
    return pl.pallas_call(
        _rwkv_out_kernel,
        grid=(T // tm,),
        in_specs=[row, row, row, row, const((1, D)), const((1, D)), const((MXU_DIM, MXU_DIM))],
        out_specs=row,
        out_shape=jax.ShapeDtypeStruct((T, D), BF16),
        compiler_params=_cparams(("parallel",)),
        name="l1_rwkv_norm_gate",
    )(yf, yb, bon, g, ln_g.reshape(1, D), ln_b.reshape(1, D), bd)


def _peer_layer(x2d, y2d, w_o, mod_l, norm_g2, w_q, subkeys, u_tab, v_tab, B, S):
    x1, ht, st = _post(x2d, y2d, w_o, mod_l, norm_g2, w_q, subkeys, B, S)
    r1, g1, n0, f0 = _topk(st)
    return _peer_dense(x1, mod_l, ht, u_tab, v_tab, r1, g1, n0, f0, B, S)


def kernel(x, c, ada_w, ada_b, norm_g, attn_w_in, attn_qk_g, diff_lambda, diff_subln_g, attn_w_out, rel_bias,
           rwkv_mix, rwkv_w_rkv, rwkv_w0, rwkv_w1, rwkv_w2, rwkv_a0, rwkv_a1, rwkv_a2, rwkv_g1, rwkv_g2,
           rwkv_k_k, rwkv_k_a, rwkv_r_k, rwkv_ln_g, rwkv_ln_b, rwkv_w_o,
           peer_w_q, peer_subkeys, peer_u, peer_v):
    B, S, D = x.shape
    mod = _ada(c, ada_w, ada_b)
    x2d = x.reshape(B * S, D)
    for layer in range(DEPTH):
        j = layer // 2
        mod_l = mod[layer]
        if layer % 2 == 0:
            lambda_init = 0.8 - 0.6 * math.exp(-0.3 * layer)
            qa, ka, va, qb, kb, vb = _inproj(x2d, mod_l, norm_g[layer, 0], attn_w_in[j], attn_qk_g[j], B, S)
            oa = _attn_a(qa, ka, va, diff_lambda[j], diff_subln_g[j], rel_bias, lambda_init, B, S)
            ob = _attn_b(qb, kb, vb, B, S)
            y2d = jnp.concatenate([oa, ob], axis=1)
            w_o = attn_w_out[j]
        else:
            r, v, g, kn, lw, kz, az, bon = _rwkv_pre(
                x2d, mod_l, norm_g[layer, 0], rwkv_mix[j], rwkv_w_rkv[j], rwkv_w0[j], rwkv_w1[j], rwkv_w2[j],
                rwkv_a0[j], rwkv_a1[j], rwkv_a2[j], rwkv_g1[j], rwkv_g2[j], rwkv_k_k[j], rwkv_k_a[j], rwkv_r_k[j], B, S)
            yf, yb = _rwkv_scan(r, v, kn, lw, kz, az, B, S)
            y2d = _rwkv_out(yf, yb, bon, g, rwkv_ln_g[j], rwkv_ln_b[j])
            w_o = rwkv_w_o[j]
        x2d = _peer_layer(x2d, y2d, w_o, mod_l, norm_g[layer, 1], peer_w_q[layer], peer_subkeys[layer],
                          peer_u[layer], peer_v[layer], B, S)
    return x2d.reshape(B, S, D)
```

```python
import functools
import math

import jax
import jax.numpy as jnp
from jax import lax
from jax.experimental import pallas as pl
from jax.experimental.pallas import tpu as pltpu

F32 = jnp.float32
BF16 = jnp.bfloat16
HIGHEST = lax.Precision.HIGHEST

D_MODEL = 1024
DEPTH = 2
HEAD_DIM = 64
A_HEADS = 4
A_MAPS = 8
B_Q_HEADS = 8
B_KV_HEADS = 2
GRID_W = 64
ROPE_AXIS_DIM = 32
ROPE_THETA = 10000.0
REL_BUCKETS = 32
REL_MAX_DIST = 128
RWKV_HEAD = 64
GN_EPS = 64e-5
PEER_HEADS = 8
N_KEYS = 128
PEER_TOPK = 16
NORM_EPS = 1e-6

LANES = 128
MXU_DIM = 256
VMEM_LIMIT = 56 * 1024 * 1024

NT_DIMS = (((1,), (1,)), ((), ()))


def _cparams(sem):
    return pltpu.CompilerParams(dimension_semantics=sem, vmem_limit_bytes=VMEM_LIMIT)


def _dot(a, b):
    return jnp.dot(a, b, preferred_element_type=F32)


def _dot_nt(a, b):
    return lax.dot_general(a, b, NT_DIMS, preferred_element_type=F32)


def _segsum(v, bd):
    outs = []
    n = v.shape[1]
    for c0 in range(0, n, MXU_DIM):
        w = min(MXU_DIM, n - c0)
        blk = v[:, c0:c0 + w]
        hi = blk.astype(BF16)
        lo = (blk - hi.astype(F32)).astype(BF16)
        b = bd[:w, :w]
        outs.append(_dot(hi, b) + _dot(lo, b))
    return outs[0] if len(outs) == 1 else jnp.concatenate(outs, axis=1)


def _modnorm(x, g, sh, sc):
    ms = jnp.mean(x * x, axis=-1, keepdims=True)
    return (x * lax.rsqrt(ms + NORM_EPS) * g) * (1.0 + sc) + sh


def _ada_kernel(c_ref, w_ref, b_ref, o_ref):
    c = c_ref[...]
    cs = c * (1.0 / (1.0 + jnp.exp(-c)))
    o_ref[...] = jnp.dot(cs, w_ref[...], preferred_element_type=F32, precision=HIGHEST) + b_ref[...]


def _ada(c, ada_w, ada_b):
    B, D = c.shape
    n6 = ada_w.shape[-1]
    tn = 1536
    out = pl.pallas_call(
        _ada_kernel,
        grid=(DEPTH, n6 // tn),
        in_specs=[
            pl.BlockSpec((B, D), lambda l, j: (0, 0)),
            pl.BlockSpec((None, D, tn), lambda l, j: (l, 0, j)),
            pl.BlockSpec((None, 1, tn), lambda l, j: (l, 0, j)),
        ],
        out_specs=pl.BlockSpec((None, B, tn), lambda l, j: (l, 0, j)),
        out_shape=jax.ShapeDtypeStruct((DEPTH, B, n6), F32),
        compiler_params=_cparams(("parallel", "parallel")),
        name="ada_mod",
    )(c, ada_w, ada_b.reshape(DEPTH, 1, n6))
    return out.reshape(DEPTH, B, 6, D)


C_QA, C_KA, C_VA, C_QB, C_KB, C_VB, C_QBS, C_KBS, C_END = 0, 512, 1024, 1536, 2048, 2304, 2560, 3072, 3328


def _inproj_kernel(x_ref, mod_ref, g_ref, w_ref, gain_ref, cos_ref, sin_ref, bd_ref,
                   qa_ref, ka_ref, va_ref, qb_ref, kb_ref, vb_ref):
    h = _modnorm(x_ref[...], g_ref[...], mod_ref[0:1, :], mod_ref[1:2, :])
    y = _dot(h.astype(BF16), w_ref[...])
    bd = bd_ref[...]
    gain = gain_ref[...]

    def rinv(sec):
        return lax.rsqrt(_segsum(sec * sec, bd) * (1.0 / HEAD_DIM) + NORM_EPS)

    yqa = y[:, C_QA:C_KA]
    qa_ref[...] = (yqa * rinv(yqa) * gain[:, C_QA:C_KA]).astype(BF16)
    yka = y[:, C_KA:C_VA]
    ka_ref[...] = (yka * rinv(yka) * gain[:, C_KA:C_VA]).astype(BF16)
    va_ref[...] = y[:, C_VA:C_QB].astype(BF16)
    cos = cos_ref[...]
    sin = sin_ref[...]
    yqb = y[:, C_QB:C_KB]
    qb = (yqb * gain[:, C_QB:C_KB] * cos + y[:, C_QBS:C_KBS] * gain[:, C_QBS:C_KBS] * sin) * rinv(yqb)
    qb_ref[...] = qb.astype(BF16)
    ykb = y[:, C_KB:C_VB]
    kb = (ykb * gain[:, C_KB:C_VB] * cos[:, :256] + y[:, C_KBS:C_END] * gain[:, C_KBS:C_END] * sin[:, :256]) * rinv(ykb)
    kb_ref[...] = kb.astype(BF16)
    vb_ref[...] = y[:, C_VB:C_QBS].astype(BF16)


def _swap16(a):
    sh = a.shape
    a = a.reshape(sh[:-1] + (sh[-1] // 32, 2, 16))
    return a[..., ::-1, :].reshape(sh)


def _inproj(x2d, mod_l, norm_g, w_in, qk_g, B, S):
    T, D = x2d.shape
    tm = 256
    nper = S // tm
    scale = HEAD_DIM ** -0.5
    wqa, wka, wva, wqb, wkb, wvb = (w_in[:, a:b] for a, b in
                                    ((0, 512), (512, 1024), (1024, 1536), (1536, 2048), (2048, 2176), (2176, 2304)))
    dup = lambda w: jnp.repeat(w.reshape(D, B_KV_HEADS, 1, HEAD_DIM), 2, axis=2).reshape(D, 4 * HEAD_DIM)
    wkbd = dup(wkb)
    w_all = jnp.concatenate([wqa, wka, wva, wqb, wkbd, dup(wvb), _swap16(wqb), _swap16(wkbd)], axis=1).astype(BF16)
    gq = jnp.tile(qk_g[2], B_Q_HEADS) * scale
    gk = jnp.tile(qk_g[3], 4)
    gain = jnp.concatenate([jnp.tile(qk_g[0], A_MAPS) * scale, jnp.tile(qk_g[1], A_MAPS), jnp.ones((512,), F32),
                            gq, gk, jnp.ones((256,), F32), _swap16(gq), _swap16(gk)]).reshape(1, C_END)
    pos = jnp.arange(S)
    row = (pos // GRID_W).astype(F32)
    col = (pos % GRID_W).astype(F32)
    inv = ROPE_THETA ** (-jnp.arange(0, ROPE_AXIS_DIM, 2, dtype=F32) / ROPE_AXIS_DIM)
    ang = jnp.concatenate([row[:, None] * inv] * 2 + [col[:, None] * inv] * 2, axis=1)
    sign = jnp.tile(jnp.concatenate([-jnp.ones((16,), F32), jnp.ones((16,), F32)]), 2)
    cos_t = jnp.tile(jnp.cos(ang), (1, B_Q_HEADS))
    sin_t = jnp.tile(jnp.sin(ang) * sign, (1, B_Q_HEADS))
    bd = jnp.kron(jnp.eye(MXU_DIM // HEAD_DIM, dtype=F32), jnp.ones((HEAD_DIM, HEAD_DIM), F32)).astype(BF16)
    row_spec = lambda w: pl.BlockSpec((tm, w), lambda i: (i, 0))
    const = lambda shp: pl.BlockSpec(shp, lambda i: (0,) * len(shp))
    outs = pl.pallas_call(
        _inproj_kernel,
        grid=(T // tm,),
        in_specs=[
            row_spec(D),
            pl.BlockSpec((None, 6, D), lambda i: (i // nper, 0, 0)),
            const((1, D)),
            const((D, C_END)),
            const((1, C_END)),
            pl.BlockSpec((tm, 512), lambda i: (i % nper, 0)),
            pl.BlockSpec((tm, 512), lambda i: (i % nper, 0)),
            const((MXU_DIM, MXU_DIM)),
        ],
        out_specs=[row_spec(512), row_spec(512), row_spec(512), row_spec(512), row_spec(256), row_spec(256)],
        out_shape=[jax.ShapeDtypeStruct((T, w), BF16) for w in (512, 512, 512, 512, 256, 256)],
        compiler_params=_cparams(("parallel",)),
        name="l0_inproj",
    )(x2d, mod_l, norm_g.reshape(1, D), w_all, gain, cos_t, sin_t, bd)
    return outs


def _t5_bucket(rel):
    nb = REL_BUCKETS // 2
    max_exact = nb // 2
    ret = jnp.where(rel > 0, nb, 0)
    n = jnp.abs(rel)
    nf = jnp.maximum(n, 1).astype(F32)
    large = max_exact + (jnp.log(nf / max_exact) / math.log(REL_MAX_DIST / max_exact) * (nb - max_exact)).astype(jnp.int32)
    large = jnp.minimum(large, nb - 1)
    return ret + jnp.where(n < max_exact, n, large)


def _softmax_rows(s):
    m = jnp.max(s, axis=1, keepdims=True)
    e = jnp.exp(s - m)
    l = jnp.sum(e, axis=1, keepdims=True)
    return e * (1.0 / l)


def _attn_a_kernel(lam_ref, q_ref, k_ref, v_ref, bias_ref, sg_ref, o_ref, *, tq, S, lambda_init):
    i = pl.program_id(2)
    lv = lam_ref[...]
    lam = (jnp.exp(jnp.sum(lv[0:1] * lv[1:2], axis=1, keepdims=True))
           - jnp.exp(jnp.sum(lv[2:3] * lv[3:4], axis=1, keepdims=True)) + lambda_init)
    q = q_ref[...]
    k = k_ref[...]
    lane = lax.broadcasted_iota(jnp.int32, q.shape, 1)
    nsl = tq // LANES + 4
    ps = []
    for mi in range(2):
        qm = jnp.where((lane < HEAD_DIM) == (mi == 0), q, jnp.zeros_like(q))
        s = _dot_nt(qm, k)
        tiles = []
        for j in range(S // LANES):
            idx = jnp.clip(j - i * (tq // LANES) + 2, 0, nsl - 1)
            tiles.append(bias_ref[mi, idx])
        ps.append(_softmax_rows(s + jnp.concatenate(tiles, axis=1)))
    diff = (ps[0] - lam * ps[1]).astype(BF16)
    oa = _dot(diff, v_ref[...])
    ms = jnp.mean(oa * oa, axis=1, keepdims=True)
    o_ref[...] = (oa * lax.rsqrt(ms + NORM_EPS) * sg_ref[...] * (1.0 - lambda_init)).astype(BF16)


def _attn_a(qa, ka, va, diff_lambda, subln_g, rel_bias, lambda_init, B, S):
    tq = 256
    nq = S // tq
    nsl = tq // LANES + 4
    d = jnp.arange(nsl) - 2
    rel = d[:, None, None] * LANES + jnp.arange(LANES)[None, None, :] - jnp.arange(tq)[None, :, None]
    rel = jnp.where(d[:, None, None] < -1, -S, jnp.where(d[:, None, None] > tq // LANES, S, rel))
    onehot = (_t5_bucket(rel)[None] == jnp.arange(REL_BUCKETS)[:, None, None, None]).astype(F32)
    bias_tab = jnp.einsum("bm,bsqc->msqc", rel_bias.astype(F32), onehot, precision=HIGHEST)
    kern = functools.partial(_attn_a_kernel, tq=tq, S=S, lambda_init=lambda_init)
    return pl.pallas_call(
        kern,
        grid=(B, A_HEADS, nq),
        in_specs=[
            pl.BlockSpec((4, HEAD_DIM), lambda b, h, i: (0, 0)),
            pl.BlockSpec((tq, LANES), lambda b, h, i: (b * nq + i, h)),
            pl.BlockSpec((S, LANES), lambda b, h, i: (b, h)),
            pl.BlockSpec((S, LANES), lambda b, h, i: (b, h)),
            pl.BlockSpec((2, nsl, tq, LANES), lambda b, h, i: (h, 0, 0, 0)),
            pl.BlockSpec((1, LANES), lambda b, h, i: (0, 0)),
        ],
        out_specs=pl.BlockSpec((tq, LANES), lambda b, h, i: (b * nq + i, h)),
        out_shape=jax.ShapeDtypeStruct((B * S, A_HEADS * LANES), BF16),
        compiler_params=_cparams(("parallel", "parallel", "arbitrary")),
        name="l0_attn_diff",
    )(diff_lambda, qa, ka, va, bias_tab, subln_g.reshape(1, LANES))


def _attn_b_kernel(q_ref, k_ref, v_ref, o_ref):
    q = q_ref[...]
    k = k_ref[...]
    v = v_ref[...]
    lane = lax.broadcasted_iota(jnp.int32, (q.shape[0], LANES), 1)
    outs = []
    for u in range(2):
        qp = q[:, u * LANES:(u + 1) * LANES]
        pair = []
        for w in range(2):
            qm = jnp.where((lane < HEAD_DIM) == (w == 0), qp, jnp.zeros_like(qp))
            p = _softmax_rows(_dot_nt(qm, k)).astype(BF16)
            pair.append(_dot(p, v))
        outs.append(jnp.where(lane < HEAD_DIM, pair[0], pair[1]))
    o_ref[...] = jnp.concatenate(outs, axis=1).astype(BF16)


def _attn_b(qb, kb, vb, B, S):
    tq = 256
    nq = S // tq
    return pl.pallas_call(
        _attn_b_kernel,
        grid=(B, B_KV_HEADS, nq),
        in_specs=[
            pl.BlockSpec((tq, 2 * LANES), lambda b, g, i: (b * nq + i, g)),
            pl.BlockSpec((S, LANES), lambda b, g, i: (b, g)),
            pl.BlockSpec((S, LANES), lambda b, g, i: (b, g)),
        ],
        out_specs=pl.BlockSpec((tq, 2 * LANES), lambda b, g, i: (b * nq + i, g)),
        out_shape=jax.ShapeDtypeStruct((B * S, B_Q_HEADS * HEAD_DIM), BF16),
        compiler_params=_cparams(("parallel", "parallel", "arbitrary")),
        name="l0_attn_gqa",
    )(qb, kb, vb)


def _post_kernel(x_ref, ya_ref, yb_ref, wa_ref, wb_ref, mod_ref, g_ref, wq_ref, sk_ref, x1_ref, ht_ref, st_ref):
    y = _dot(ya_ref[...], wa_ref[...]) + _dot(yb_ref[...], wb_ref[...])
    x1 = x_ref[...] + mod_ref[2:3, :] * y
    x1_ref[...] = x1
    h2 = _modnorm(x1, g_ref[...], mod_ref[3:4, :], mod_ref[4:5, :])
    ht_ref[...] = h2.T.astype(BF16)
    q = _dot(h2.astype(BF16), wq_ref[...])
    for hp in range(2 * PEER_HEADS):
        qs = q[:, hp * N_KEYS:(hp + 1) * N_KEYS].astype(BF16)
        st_ref[hp] = _dot_nt(sk_ref[hp], qs)


def _post(x2d, y2d, w_o, mod_l, norm_g, w_q, subkeys, B, S):
    T, D = x2d.shape
    tm = 256
    nper = S // tm
    half = y2d.shape[1] // 2
    nq = w_q.shape[1]
    w_o = w_o.astype(BF16)
    sk = subkeys.reshape(2 * PEER_HEADS, N_KEYS, N_KEYS).astype(BF16)
    const = lambda shp: pl.BlockSpec(shp, lambda i: (0,) * len(shp))
    return pl.pallas_call(
        _post_kernel,
        grid=(T // tm,),
        in_specs=[
            pl.BlockSpec((tm, D), lambda i: (i, 0)),
            pl.BlockSpec((tm, half), lambda i: (i, 0)),
            pl.BlockSpec((tm, half), lambda i: (i, 1)),
            const((half, D)),
            const((half, D)),
            pl.BlockSpec((None, 6, D), lambda i: (i // nper, 0, 0)),
            const((1, D)),
            const((D, nq)),
            const((2 * PEER_HEADS, N_KEYS, N_KEYS)),
        ],
        out_specs=[
            pl.BlockSpec((tm, D), lambda i: (i, 0)),
            pl.BlockSpec((D, tm), lambda i: (0, i)),
            pl.BlockSpec((2 * PEER_HEADS, N_KEYS, tm), lambda i: (0, 0, i)),
        ],
        out_shape=[
            jax.ShapeDtypeStruct((T, D), F32),
            jax.ShapeDtypeStruct((D, T), BF16),
            jax.ShapeDtypeStruct((2 * PEER_HEADS, N_KEYS, T), F32),
        ],
        compiler_params=_cparams(("parallel",)),
        name="mix_out_peer_q",
    )(x2d, y2d, y2d, w_o[:half], w_o[half:], mod_l, norm_g.reshape(1, D), w_q.astype(BF16), sk)


_STAIR = [(i, j) for i in range(PEER_TOPK) for j in range(PEER_TOPK) if (i + 1) * (j + 1) <= PEER_TOPK]
_STAIR_ROWS = 56
_NEG = float("-inf")
WDT = BF16


def _extract_top(s, iota, nsel, sentinel, exact):
    vals = []
    rank = jnp.full(s.shape, float(nsel), F32)
    for r in range(nsel):
        m = jnp.max(s, axis=0, keepdims=True)
        hit = s == m
        if exact:
            hit = iota == jnp.min(jnp.where(hit, iota, sentinel), axis=0, keepdims=True)
        vals.append(m)
        rank = jnp.where(hit, float(r), rank)
        s = jnp.where(hit, _NEG, s)
    return vals, rank


def _peer_head_select(s0, s1, iota, iota_c, group, exact):
    a, rank0 = _extract_top(s0, iota, PEER_TOPK, float(N_KEYS), exact)
    b, rank1 = _extract_top(s1, iota, PEER_TOPK, float(N_KEYS), exact)
    cand = jnp.full(iota_c.shape, _NEG, F32)
    for p, (i, j) in enumerate(_STAIR):
        cand = jnp.where(iota_c == float(p), a[i] + b[j], cand)
    _, crank = _extract_top(cand, iota_c, PEER_TOPK, float(_STAIR_ROWS), exact)
    sel = jnp.where(crank < float(PEER_TOPK), 1.0, 0.0)
    z = jnp.sum(sel * jnp.exp(cand - (a[0] + b[0])), axis=0, keepdims=True)
    n0 = jnp.zeros(s0.shape, F32)
    for i in range(PEER_TOPK):
        cnt = jnp.sum(jnp.where(group == float(i), sel, 0.0), axis=0, keepdims=True)
        n0 = jnp.where(rank0 == float(i), cnt, n0)
    marked = (jnp.sum(jnp.where(rank0 < float(PEER_TOPK), 1.0, 0.0), axis=0, keepdims=True)
              + jnp.sum(jnp.where(rank1 < float(PEER_TOPK), 1.0, 0.0), axis=0, keepdims=True)
              + jnp.sum(sel, axis=0, keepdims=True))
    clean = jnp.max(jnp.abs(marked - 3.0 * PEER_TOPK)) == 0.0
    return rank1, jnp.exp(s1 - b[0]), n0, jnp.exp(s0 - a[0]) * (1.0 / z), clean


def _topk_kernel(st_ref, r1_ref, g1_ref, n0_ref, f0_ref, *, tt):
    iota = lax.broadcasted_iota(jnp.int32, (N_KEYS, tt), 0).astype(F32)
    iota_c = lax.broadcasted_iota(jnp.int32, (_STAIR_ROWS, tt), 0).astype(F32)
    starts = [p for p, (i, j) in enumerate(_STAIR) if j == 0]
    group = jnp.zeros((_STAIR_ROWS, tt), F32)
    for st in starts[1:]:
        group = group + jnp.where(iota_c >= float(st), 1.0, 0.0)

    def store(h, rank1, g1, n0, f0):
        r1_ref[h] = rank1.astype(WDT)
        g1_ref[h] = g1.astype(WDT)
        n0_ref[h] = n0
        f0_ref[h] = f0

    def head_body(h, carry):
        s0 = st_ref[2 * h]
        s1 = st_ref[2 * h + 1]
        rank1, g1, n0, f0, clean = _peer_head_select(s0, s1, iota, iota_c, group, exact=False)
        store(h, rank1, g1, n0, f0)

        @pl.when(jnp.logical_not(clean))
        def _():
            store(h, *_peer_head_select(s0, s1, iota, iota_c, group, exact=True)[:4])

        return carry

    lax.fori_loop(0, PEER_HEADS, head_body, 0)


def _topk(st):
    T = st.shape[-1]
    tt = 256
    blk = pl.BlockSpec((PEER_HEADS, N_KEYS, tt), lambda i: (0, 0, i))
    return pl.pallas_call(
        functools.partial(_topk_kernel, tt=tt),
        grid=(T // tt,),
        in_specs=[pl.BlockSpec((2 * PEER_HEADS, N_KEYS, tt), lambda i: (0, 0, i))],
        out_specs=[blk, blk, blk, blk],
        out_shape=[jax.ShapeDtypeStruct((PEER_HEADS, N_KEYS, T), WDT)] * 2
                  + [jax.ShapeDtypeStruct((PEER_HEADS, N_KEYS, T), F32)] * 2,
        compiler_params=_cparams(("parallel",)),
        name="peer_topk",
    )(st)


_GELU_K0 = -2.0 * math.sqrt(2.0 / math.pi) * math.log2(math.e)
_GELU_K1 = _GELU_K0 * 0.044715


def _gelu_tanh(x):
    return x / (1.0 + jnp.exp2(x * (_GELU_K0 + _GELU_K1 * (x * x))))


def _peer_dense_kernel(x_ref, mod_ref, ht_ref, u_ref, v_ref, r1_ref, g1_ref, n0_ref, f0_ref, o_ref, acc_ref, z_ref,
                       *, tb, eb):
    e = pl.program_id(1)

    @pl.when(e == 0)
    def _():
        acc_ref[...] = jnp.zeros_like(acc_ref)

    act = _dot(u_ref[...], ht_ref[...])
    for ii in range(eb // N_KEYS):
        w = jnp.zeros((N_KEYS, tb), WDT)
        for h in range(PEER_HEADS):
            n0 = n0_ref[h, ii:ii + 1, :].astype(WDT)
            f0 = f0_ref[h, ii:ii + 1, :].astype(WDT)
            w = w + jnp.where(r1_ref[h] < n0, g1_ref[h] * f0, jnp.zeros((), WDT))
        rs = slice(ii * N_KEYS, (ii + 1) * N_KEYS)
        z_ref[rs, :] = w * _gelu_tanh(act[rs, :]).astype(WDT)
    acc_ref[...] += lax.dot_general(z_ref[...], v_ref[...], (((0,), (0,)), ((), ())), preferred_element_type=F32)

    @pl.when(e == pl.num_programs(1) - 1)
    def _():
        o_ref[...] = x_ref[...] + mod_ref[5:6, :] * acc_ref[...]


def _peer_dense(x2d, mod_l, ht, u_tab, v_tab, r1, g1, n0, f0, B, S):
    T, D = x2d.shape
    E = u_tab.shape[0]
    tb = min(512, S)
    eb = 1024
    nper = S // tb
    tok = pl.BlockSpec((PEER_HEADS, N_KEYS, tb), lambda t, e: (0, 0, t))
    rows = pl.BlockSpec((PEER_HEADS, eb // N_KEYS, tb), lambda t, e: (0, e, t))
    return pl.pallas_call(
        functools.partial(_peer_dense_kernel, tb=tb, eb=eb),
        grid=(T // tb, E // eb),
        in_specs=[
            pl.BlockSpec((tb, D), lambda t, e: (t, 0)),
            pl.BlockSpec((None, 6, D), lambda t, e: (t // nper, 0, 0)),
            pl.BlockSpec((D, tb), lambda t, e: (0, t)),
            pl.BlockSpec((eb, D), lambda t, e: (e, 0)),
            pl.BlockSpec((eb, D), lambda t, e: (e, 0)),
            tok, tok, rows, rows,
        ],
        out_specs=pl.BlockSpec((tb, D), lambda t, e: (t, 0)),
        out_shape=jax.ShapeDtypeStruct((T, D), F32),
        scratch_shapes=[pltpu.VMEM((tb, D), F32), pltpu.VMEM((eb, tb), WDT)],
        compiler_params=_cparams(("parallel", "arbitrary")),
        name="peer_dense",
    )(x2d, mod_l, ht, u_tab.astype(BF16), v_tab.astype(BF16), r1, g1, n0, f0)


def _rwkv_pre_kernel(x_ref, xp_ref, xn_ref, mod_ref, g_ref, mix_ref, wr_ref, wk_ref, wv_ref, g1_ref, g2_ref,
                     w1_ref, w2_ref, a1_ref, a2_ref, w0_ref, a0_ref, kk_ref, ka_ref, rk_ref, bd_ref,
                     r_o, v_o, g_o, kn_o, lw_o, kz_o, az_o, bon_o, *, nper):
    i = pl.program_id(0)
    gn = g_ref[...]
    sh = mod_ref[0:1, :]
    sc = mod_ref[1:2, :]
    h = _modnorm(x_ref[...], gn, sh, sc)
    tm = h.shape[0]
    hp = _modnorm(xp_ref[...], gn, sh, sc)[7:8, :]
    hn = _modnorm(xn_ref[...], gn, sh, sc)[0:1, :]
    hp = jnp.where(i % nper == 0, 0.0, hp)
    hn = jnp.where(i % nper == nper - 1, 0.0, hn)
    rows = lax.broadcasted_iota(jnp.int32, h.shape, 0)
    prev = jnp.where(rows == 0, hp, pltpu.roll(h, 1, 0))
    nxt = jnp.where(rows == tm - 1, hn, pltpu.roll(h, tm - 1, 0))
    xx = 0.5 * (prev + nxt) - h
    mix = mix_ref[...]
    xr, xw, xk, xv, xa, xg = ((h + xx * mix[j:j + 1, :]).astype(BF16) for j in range(6))
    r = _dot(xr, wr_ref[...])
    k = _dot(xk, wk_ref[...])
    v = _dot(xv, wv_ref[...])
    gl = _dot(xg, g1_ref[...])
    g = _dot((1.0 / (1.0 + jnp.exp(-gl))).astype(BF16), g2_ref[...])
    wl = _dot(jnp.tanh(_dot(xw, w1_ref[...])).astype(BF16), w2_ref[...])
    al = _dot(_dot(xa, a1_ref[...]).astype(BF16), a2_ref[...])
    bd = bd_ref[...]
    kraw = k * kk_ref[...]
    kn = kraw / jnp.maximum(jnp.sqrt(_segsum(kraw * kraw, bd)), 1e-12)
    r_o[...] = r
    v_o[...] = v
    g_o[...] = g
    kn_o[...] = kn
    D = r.shape[1]
    bon = jnp.zeros_like(r)
    for z in range(2):
        u = -(w0_ref[z:z + 1, :] + wl[:, z * D:(z + 1) * D])
        softplus = jnp.maximum(u, 0.0) + jnp.log(1.0 + jnp.exp(-jnp.abs(u)))
        lw_o[z] = -jnp.exp(-softplus - 0.5)
        a = 1.0 / (1.0 + jnp.exp(-(a0_ref[z:z + 1, :] + al[:, z * D:(z + 1) * D])))
        kz = k * (1.0 + (a - 1.0) * ka_ref[...])
        kz_o[z] = kz
        az_o[z] = a
        bon = bon + _segsum(r * kz * rk_ref[...], bd) * v
    bon_o[...] = bon


def _blockdiag2(w):
    n, D = w.shape[1], w.shape[2]
    z = jnp.zeros((n, D), w.dtype)
    return jnp.concatenate([jnp.concatenate([w[0], z], axis=1), jnp.concatenate([z, w[1]], axis=1)], axis=0)


def _rwkv_pre(x2d, mod_l, norm_g, mix, w_rkv, w0, w1, w2, a0, a1, a2, g1, g2, k_k, k_a, r_k, B, S):
    T, D = x2d.shape
    tm = 256
    nper = S // tm
    nblk8 = T // 8
    bd = jnp.kron(jnp.eye(MXU_DIM // RWKV_HEAD, dtype=F32), jnp.ones((RWKV_HEAD, RWKV_HEAD), F32)).astype(BF16)
    w1c = jnp.concatenate([w1[0], w1[1]], axis=1).astype(BF16)
    a1c = jnp.concatenate([a1[0], a1[1]], axis=1).astype(BF16)
    w2b = _blockdiag2(w2).astype(BF16)
    a2b = _blockdiag2(a2).astype(BF16)
    const = lambda shp: pl.BlockSpec(shp, lambda i: (0,) * len(shp), pipeline_mode=pl.Buffered(1))
    row = pl.BlockSpec((tm, D), lambda i: (i, 0))
    row2 = pl.BlockSpec((2, tm, D), lambda i: (0, i, 0))
    return pl.pallas_call(
        functools.partial(_rwkv_pre_kernel, nper=nper),
        grid=(T // tm,),
        in_specs=[
            row,
            pl.BlockSpec((8, D), lambda i: (jnp.maximum(i * (tm // 8) - 1, 0), 0)),
            pl.BlockSpec((8, D), lambda i: (jnp.minimum((i + 1) * (tm // 8), nblk8 - 1), 0)),
            pl.BlockSpec((None, 6, D), lambda i: (i // nper, 0, 0)),
            const((1, D)), const((6, D)),
            const((D, D)), const((D, D)), const((D, D)),
            const(g1.shape), const(g2.shape),
            const(w1c.shape), const(w2b.shape), const(a1c.shape), const(a2b.shape),
            const((2, D)), const((2, D)), const((1, D)), const((1, D)), const((1, D)),
            const((MXU_DIM, MXU_DIM)),
        ],
        out_specs=[row, row, row, row, row2, row2, row2, row],
        out_shape=[jax.ShapeDtypeStruct((T, D), F32)] * 4 + [jax.ShapeDtypeStruct((2, T, D), F32)] * 3
                  + [jax.ShapeDtypeStruct((T, D), F32)],
        compiler_params=_cparams(("parallel",)),
        name="l1_rwkv_proj",
    )(x2d, x2d, x2d, mod_l, norm_g.reshape(1, D), mix,
      w_rkv[0].astype(BF16), w_rkv[1].astype(BF16), w_rkv[2].astype(BF16), g1.astype(BF16), g2.astype(BF16),
      w1c, w2b, a1c, a2b, w0, a0, k_k.reshape(1, D), k_a.reshape(1, D), r_k.reshape(1, D), bd)


SCAN_C = 64
SCAN_W = 256
SCAN_HEADS = SCAN_W // RWKV_HEAD


def _scan_chunk(r, v, kn, lw, kz, az, S, reverse):
    C = SCAN_C
    n = SCAN_HEADS * C
    ti = lax.broadcasted_iota(jnp.int32, (C, C), 0)
    si = lax.broadcasted_iota(jnp.int32, (C, C), 1)
    inc = ((si >= ti) if reverse else (si <= ti)).astype(F32)
    cum = jnp.dot(inc, lw, preferred_element_type=F32, precision=HIGHEST)
    tot = cum[0:1, :] if reverse else cum[C - 1:C, :]
    a = -kn
    b = kn * az
    pinv = jnp.exp(-cum)
    pend = jnp.exp(tot - cum)
    rt = r * jnp.exp(cum)
    at = a * jnp.exp(cum - lw)
    bt = (b * pinv).astype(BF16)
    kt = (kz * pinv).astype(BF16)

    lane = lax.broadcasted_iota(jnp.int32, (C, SCAN_W), 1) // RWKV_HEAD
    stack = lambda t: jnp.concatenate([jnp.where(lane == hh, t, 0.0) for hh in range(SCAN_HEADS)], axis=0)
    tile = lambda t: jnp.concatenate([t] * SCAN_HEADS, axis=0)
    ri = lax.broadcasted_iota(jnp.int32, (n, n), 0)
    ci = lax.broadcasted_iota(jnp.int32, (n, n), 1)
    same = (ri // C) == (ci // C)
    tt = ri % C
    ss = ci % C
    m_str = same & ((ss > tt) if reverse else (ss < tt))
    m_inc = same & ((ss >= tt) if reverse else (ss <= tt))

    at_s = stack(at).astype(BF16)
    rt_s = stack(rt).astype(BF16)
    bt_t = tile(bt)
    kt_t = tile(kt)
    a_ab = jnp.where(m_str, _dot_nt(at_s, bt_t), 0.0)
    a_ak = jnp.where(m_str, _dot_nt(at_s, kt_t), 0.0)
    a_rb = jnp.where(m_inc, _dot_nt(rt_s, bt_t), 0.0)
    a_rk = jnp.where(m_inc, _dot_nt(rt_s, kt_t), 0.0)

    sb = S.astype(BF16)
    x0 = _dot_nt(jnp.concatenate([at, rt], axis=0).astype(BF16), sb)
    a_s0 = x0[:C]
    r_s0 = x0[C:]
    v_t = tile(v).astype(BF16)
    x = tile(a_s0) + _dot(a_ak.astype(BF16), v_t)
    ap = a_ab
    nsteps = int(math.log2(C))
    for it in range(nsteps):
        apb = ap.astype(BF16)
        x = x + _dot(apb, x.astype(BF16))
        if it + 1 < nsteps:
            ap = _dot(apb, apb)
    y_s = _dot(a_rb.astype(BF16), x.astype(BF16)) + _dot(a_rk.astype(BF16), v_t)
    unstack = lambda t: sum(jnp.where(lane == hh, t[hh * C:(hh + 1) * C], 0.0) for hh in range(SCAN_HEADS))
    u = unstack(x)
    y = r_s0 + unstack(y_s)
    uv = jnp.concatenate([u, v], axis=0)
    bk = jnp.concatenate([b * pend, kz * pend], axis=0)
    upd = _dot(uv.T.astype(BF16), bk.astype(BF16))
    vi = lax.broadcasted_iota(jnp.int32, (SCAN_W, SCAN_W), 0) // RWKV_HEAD
    ki = lax.broadcasted_iota(jnp.int32, (SCAN_W, SCAN_W), 1) // RWKV_HEAD
    s_new = S * jnp.exp(tot) + jnp.where(vi == ki, upd, 0.0)
    return y, s_new


def _scan_kernel(rf, vf, knf, lwf, kzf, azf, rb, vb, knb, lwb, kzb, azb, yf_o, yb_o, s_ref):
    @pl.when(pl.program_id(2) == 0)
    def _():
        s_ref[...] = jnp.zeros_like(s_ref)

    y, s_new = _scan_chunk(rf[...], vf[...], knf[...], lwf[...], kzf[...], azf[...], s_ref[0], False)
    yf_o[...] = y
    s_ref[0] = s_new
    y, s_new = _scan_chunk(rb[...], vb[...], knb[...], lwb[...], kzb[...], azb[...], s_ref[1], True)
    yb_o[...] = y
    s_ref[1] = s_new


def _rwkv_scan(r, v, kn, lw, kz, az, B, S):
    T, D = r.shape
    C = SCAN_C
    nc = S // C
    ng = D // SCAN_W
    fwd = lambda b, g, c: (b * nc + c, g)
    bwd = lambda b, g, c: (b * nc + nc - 1 - c, g)
    fwd3 = lambda z: (lambda b, g, c: (z, b * nc + c, g))
    bwd3 = lambda z: (lambda b, g, c: (z, b * nc + nc - 1 - c, g))
    s2 = lambda im: pl.BlockSpec((C, SCAN_W), im)
    s3 = lambda im: pl.BlockSpec((None, C, SCAN_W), im)
    return pl.pallas_call(
        _scan_kernel,
        grid=(B, ng, nc),
        in_specs=[s2(fwd), s2(fwd), s2(fwd), s3(fwd3(0)), s3(fwd3(0)), s3(fwd3(0)),
                  s2(bwd), s2(bwd), s2(bwd), s3(bwd3(1)), s3(bwd3(1)), s3(bwd3(1))],
        out_specs=[s2(fwd), s2(bwd)],
        out_shape=[jax.ShapeDtypeStruct((T, D), F32)] * 2,
        scratch_shapes=[pltpu.VMEM((2, SCAN_W, SCAN_W), F32)],
        compiler_params=_cparams(("parallel", "parallel", "arbitrary")),
        name="l1_wkv_scan",
    )(r, v, kn, lw, kz, az, r, v, kn, lw, kz, az)


def _rwkv_out_kernel(yf_ref, yb_ref, bon_ref, g_ref, lg_ref, lb_ref, bd_ref, o_ref):
    y = yf_ref[...] + yb_ref[...]
    bd = bd_ref[...]
    mu = _segsum(y, bd) * (1.0 / RWKV_HEAD)
    yc = y - mu
    var = _segsum(yc * yc, bd) * (1.0 / RWKV_HEAD)
    yn = yc * lax.rsqrt(var + GN_EPS) * lg_ref[...] + lb_ref[...]
    o_ref[...] = ((yn + bon_ref[...]) * g_ref[...]).astype(BF16)


def _rwkv_out(yf, yb, bon, g, ln_g, ln_b):
    T, D = yf.shape
    tm = 256
    bd = jnp.kron(jnp.eye(MXU_DIM // RWKV_HEAD, dtype=F32), jnp.ones((RWKV_HEAD, RWKV_HEAD), F32)).astype(BF16)
    row = pl.BlockSpec((tm, D), lambda i: (i, 0))
    const = lambda shp: pl.BlockSpec(shp, lambda i: (0,) * len(shp))
    return pl.pallas_call(
        _rwkv_out_kernel,
        grid=(T // tm,),
        in_specs=[row, row, row, row, const((1, D)), const((1, D)), const((MXU_DIM, MXU_DIM))],
        out_specs=row,
        out_shape=jax.ShapeDtypeStruct((T, D), BF16),
        compiler_params=_cparams(("parallel",)),
        name="l1_rwkv_norm_gate",
    )(yf, yb, bon, g, ln_g.reshape(1, D), ln_b.reshape(1, D), bd)


def _peer_layer(x2d, y2d, w_o, mod_l, norm_g2, w_q, subkeys, u_tab, v_tab, B, S):
    x1, ht, st = _post(x2d, y2d, w_o, mod_l, norm_g2, w_q, subkeys, B, S)
    r1, g1, n0, f0 = _topk(st)
    return _peer_dense(x1, mod_l, ht, u_tab, v_tab, r1, g1, n0, f0, B, S)


def kernel(x, c, ada_w, ada_b, norm_g, attn_w_in, attn_qk_g, diff_lambda, diff_subln_g, attn_w_out, rel_bias,
           rwkv_mix, rwkv_w_rkv, rwkv_w0, rwkv_w1, rwkv_w2, rwkv_a0, rwkv_a1, rwkv_a2, rwkv_g1, rwkv_g2,
           rwkv_k_k, rwkv_k_a, rwkv_r_k, rwkv_ln_g, rwkv_ln_b, rwkv_w_o,
           peer_w_q, peer_subkeys, peer_u, peer_v):
    B, S, D = x.shape
    mod = _ada(c, ada_w, ada_b)
    x2d = x.reshape(B * S, D)
    for layer in range(DEPTH):
        j = layer // 2
        mod_l = mod[layer]
        if layer % 2 == 0:
            lambda_init = 0.8 - 0.6 * math.exp(-0.3 * layer)
            qa, ka, va, qb, kb, vb = _inproj(x2d, mod_l, norm_g[layer, 0], attn_w_in[j], attn_qk_g[j], B, S)
            oa = _attn_a(qa, ka, va, diff_lambda[j], diff_subln_g[j], rel_bias, lambda_init, B, S)
            ob = _attn_b(qb, kb, vb, B, S)
            y2d = jnp.concatenate([oa, ob], axis=1)
            w_o = attn_w_out[j]
        else:
            r, v, g, kn, lw, kz, az, bon = _rwkv_pre(
                x2d, mod_l, norm_g[layer, 0], rwkv_mix[j], rwkv_w_rkv[j], rwkv_w0[j], rwkv_w1[j], rwkv_w2[j],
                rwkv_a0[j], rwkv_a1[j], rwkv_a2[j], rwkv_g1[j], rwkv_g2[j], rwkv_k_k[j], rwkv_k_a[j], rwkv_r_k[j], B, S)
            yf, yb = _rwkv_scan(r, v, kn, lw, kz, az, B, S)
            y2d = _rwkv_out(yf, yb, bon, g, rwkv_ln_g[j], rwkv_ln_b[j])
            w_o = rwkv_w_o[j]
        x2d = _peer_layer(x2d, y2d, w_o, mod_l, norm_g[layer, 1], peer_w_q[layer], peer_subkeys[layer],
                          peer_u[layer], peer_v[layer], B, S)
    return x2d.reshape(B, S, D)
```

```python
import functools
import math

import jax
import jax.numpy as jnp
from jax import lax
from jax.experimental import pallas as pl
from jax.experimental.pallas import tpu as pltpu

F32 = jnp.float32
BF16 = jnp.bfloat16
HIGHEST = lax.Precision.HIGHEST

D_MODEL = 1024
DEPTH = 2
HEAD_DIM = 64
A_HEADS = 4
A_MAPS = 8
B_Q_HEADS = 8
B_KV_HEADS = 2
GRID_W = 64
ROPE_AXIS_DIM = 32
ROPE_THETA = 10000.0
REL_BUCKETS = 32
REL_MAX_DIST = 128
RWKV_HEAD = 64
GN_EPS = 64e-5
PEER_HEADS = 8
N_KEYS = 128
PEER_TOPK = 16
NORM_EPS = 1e-6

LANES = 128
MXU_DIM = 256
VMEM_LIMIT = 56 * 1024 * 1024

NT_DIMS = (((1,), (1,)), ((), ()))


def _cparams(sem):
    return pltpu.CompilerParams(dimension_semantics=sem, vmem_limit_bytes=VMEM_LIMIT)


def _dot(a, b):
    return jnp.dot(a, b, preferred_element_type=F32)


def _dot_nt(a, b):
    return lax.dot_general(a, b, NT_DIMS, preferred_element_type=F32)


def _segsum(v, bd):
    outs = []
    n = v.shape[1]
    for c0 in range(0, n, MXU_DIM):
        w = min(MXU_DIM, n - c0)
        blk = v[:, c0:c0 + w]
        hi = blk.astype(BF16)
        lo = (blk - hi.astype(F32)).astype(BF16)
        b = bd[:w, :w]
        outs.append(_dot(hi, b) + _dot(lo, b))
    return outs[0] if len(outs) == 1 else jnp.concatenate(outs, axis=1)


def _modnorm(x, g, sh, sc):
    ms = jnp.mean(x * x, axis=-1, keepdims=True)
    return (x * lax.rsqrt(ms + NORM_EPS) * g) * (1.0 + sc) + sh


def _ada_kernel(c_ref, w_ref, b_ref, o_ref):
    c = c_ref[...]
    cs = c * (1.0 / (1.0 + jnp.exp(-c)))
    o_ref[...] = jnp.dot(cs, w_ref[...], preferred_element_type=F32, precision=HIGHEST) + b_ref[...]


def _ada(c, ada_w, ada_b):
    B, D = c.shape
    n6 = ada_w.shape[-1]
    tn = 1536
    out = pl.pallas_call(
        _ada_kernel,
        grid=(DEPTH, n6 // tn),
        in_specs=[
            pl.BlockSpec((B, D), lambda l, j: (0, 0)),
            pl.BlockSpec((None, D, tn), lambda l, j: (l, 0, j)),
            pl.BlockSpec((None, 1, tn), lambda l, j: (l, 0, j)),
        ],
        out_specs=pl.BlockSpec((None, B, tn), lambda l, j: (l, 0, j)),
        out_shape=jax.ShapeDtypeStruct((DEPTH, B, n6), F32),
        compiler_params=_cparams(("parallel", "parallel")),
        name="ada_mod",
    )(c, ada_w, ada_b.reshape(DEPTH, 1, n6))
    return out.reshape(DEPTH, B, 6, D)


C_QA, C_KA, C_VA, C_QB, C_KB, C_VB, C_QBS, C_KBS, C_END = 0, 512, 1024, 1536, 2048, 2304, 2560, 3072, 3328


def _inproj_kernel(x_ref, mod_ref, g_ref, w_ref, gain_ref, cos_ref, sin_ref, bd_ref,
                   qa_ref, ka_ref, va_ref, qb_ref, kb_ref, vb_ref):
    h = _modnorm(x_ref[...], g_ref[...], mod_ref[0:1, :], mod_ref[1:2, :])
    y = _dot(h.astype(BF16), w_ref[...])
    bd = bd_ref[...]
    gain = gain_ref[...]

    def rinv(sec):
        return lax.rsqrt(_segsum(sec * sec, bd) * (1.0 / HEAD_DIM) + NORM_EPS)

    yqa = y[:, C_QA:C_KA]
    qa_ref[...] = (yqa * rinv(yqa) * gain[:, C_QA:C_KA]).astype(BF16)
    yka = y[:, C_KA:C_VA]
    ka_ref[...] = (yka * rinv(yka) * gain[:, C_KA:C_VA]).astype(BF16)
    va_ref[...] = y[:, C_VA:C_QB].astype(BF16)
    cos = cos_ref[...]
    sin = sin_ref[...]
    yqb = y[:, C_QB:C_KB]
    qb = (yqb * gain[:, C_QB:C_KB] * cos + y[:, C_QBS:C_KBS] * gain[:, C_QBS:C_KBS] * sin) * rinv(yqb)
    qb_ref[...] = qb.astype(BF16)
    ykb = y[:, C_KB:C_VB]
    kb = (ykb * gain[:, C_KB:C_VB] * cos[:, :256] + y[:, C_KBS:C_END] * gain[:, C_KBS:C_END] * sin[:, :256]) * rinv(ykb)
    kb_ref[...] = kb.astype(BF16)
    vb_ref[...] = y[:, C_VB:C_QBS].astype(BF16)


def _swap16(a):
    sh = a.shape
    a = a.reshape(sh[:-1] + (sh[-1] // 32, 2, 16))
    return a[..., ::-1, :].reshape(sh)


def _inproj(x2d, mod_l, norm_g, w_in, qk_g, B, S):
    T, D = x2d.shape
    tm = 256
    nper = S // tm
    scale = HEAD_DIM ** -0.5
    wqa, wka, wva, wqb, wkb, wvb = (w_in[:, a:b] for a, b in
                                    ((0, 512), (512, 1024), (1024, 1536), (1536, 2048), (2048, 2176), (2176, 2304)))
    dup = lambda w: jnp.repeat(w.reshape(D, B_KV_HEADS, 1, HEAD_DIM), 2, axis=2).reshape(D, 4 * HEAD_DIM)
    wkbd = dup(wkb)
    w_all = jnp.concatenate([wqa, wka, wva, wqb, wkbd, dup(wvb), _swap16(wqb), _swap16(wkbd)], axis=1).astype(BF16)
    gq = jnp.tile(qk_g[2], B_Q_HEADS) * scale
    gk = jnp.tile(qk_g[3], 4)
    gain = jnp.concatenate([jnp.tile(qk_g[0], A_MAPS) * scale, jnp.tile(qk_g[1], A_MAPS), jnp.ones((512,), F32),
                            gq, gk, jnp.ones((256,), F32), _swap16(gq), _swap16(gk)]).reshape(1, C_END)
    pos = jnp.arange(S)
    row = (pos // GRID_W).astype(F32)
    col = (pos % GRID_W).astype(F32)
    inv = ROPE_THETA ** (-jnp.arange(0, ROPE_AXIS_DIM, 2, dtype=F32) / ROPE_AXIS_DIM)
    ang = jnp.concatenate([row[:, None] * inv] * 2 + [col[:, None] * inv] * 2, axis=1)
    sign = jnp.tile(jnp.concatenate([-jnp.ones((16,), F32), jnp.ones((16,), F32)]), 2)
    cos_t = jnp.tile(jnp.cos(ang), (1, B_Q_HEADS))
    sin_t = jnp.tile(jnp.sin(ang) * sign, (1, B_Q_HEADS))
    bd = jnp.kron(jnp.eye(MXU_DIM // HEAD_DIM, dtype=F32), jnp.ones((HEAD_DIM, HEAD_DIM), F32)).astype(BF16)
    row_spec = lambda w: pl.BlockSpec((tm, w), lambda i: (i, 0))
    const = lambda shp: pl.BlockSpec(shp, lambda i: (0,) * len(shp))
    outs = pl.pallas_call(
        _inproj_kernel,
        grid=(T // tm,),
        in_specs=[
            row_spec(D),
            pl.BlockSpec((None, 6, D), lambda i: (i // nper, 0, 0)),
            const((1, D)),
            const((D, C_END)),
            const((1, C_END)),
            pl.BlockSpec((tm, 512), lambda i: (i % nper, 0)),
            pl.BlockSpec((tm, 512), lambda i: (i % nper, 0)),
            const((MXU_DIM, MXU_DIM)),
        ],
        out_specs=[row_spec(512), row_spec(512), row_spec(512), row_spec(512), row_spec(256), row_spec(256)],
        out_shape=[jax.ShapeDtypeStruct((T, w), BF16) for w in (512, 512, 512, 512, 256, 256)],
        compiler_params=_cparams(("parallel",)),
        name="l0_inproj",
    )(x2d, mod_l, norm_g.reshape(1, D), w_all, gain, cos_t, sin_t, bd)
    return outs


def _t5_bucket(rel):
    nb = REL_BUCKETS // 2
    max_exact = nb // 2
    ret = jnp.where(rel > 0, nb, 0)
    n = jnp.abs(rel)
    nf = jnp.maximum(n, 1).astype(F32)
    large = max_exact + (jnp.log(nf / max_exact) / math.log(REL_MAX_DIST / max_exact) * (nb - max_exact)).astype(jnp.int32)
    large = jnp.minimum(large, nb - 1)
    return ret + jnp.where(n < max_exact, n, large)


def _softmax_rows(s):
    m = jnp.max(s, axis=1, keepdims=True)
    e = jnp.exp(s - m)
    l = jnp.sum(e, axis=1, keepdims=True)
    return e * (1.0 / l)


def _attn_a_kernel(lam_ref, q_ref, k_ref, v_ref, bias_ref, sg_ref, o_ref, *, tq, S, lambda_init):
    i = pl.program_id(2)
    lv = lam_ref[...]
    lam = (jnp.exp(jnp.sum(lv[0:1] * lv[1:2], axis=1, keepdims=True))
           - jnp.exp(jnp.sum(lv[2:3] * lv[3:4], axis=1, keepdims=True)) + lambda_init)
    q = q_ref[...]
    k = k_ref[...]
    lane = lax.broadcasted_iota(jnp.int32, q.shape, 1)
    nsl = tq // LANES + 4
    ps = []
    for mi in range(2):
        qm = jnp.where((lane < HEAD_DIM) == (mi == 0), q, jnp.zeros_like(q))
        s = _dot_nt(qm, k)
        tiles = []
        for j in range(S // LANES):
            idx = jnp.clip(j - i * (tq // LANES) + 2, 0, nsl - 1)
            tiles.append(bias_ref[mi, idx])
        ps.append(_softmax_rows(s + jnp.concatenate(tiles, axis=1)))
    diff = (ps[0] - lam * ps[1]).astype(BF16)
    oa = _dot(diff, v_ref[...])
    ms = jnp.mean(oa * oa, axis=1, keepdims=True)
    o_ref[...] = (oa * lax.rsqrt(ms + NORM_EPS) * sg_ref[...] * (1.0 - lambda_init)).astype(BF16)


def _attn_a(qa, ka, va, diff_lambda, subln_g, rel_bias, lambda_init, B, S):
    tq = 256
    nq = S // tq
    nsl = tq // LANES + 4
    d = jnp.arange(nsl) - 2
    rel = d[:, None, None] * LANES + jnp.arange(LANES)[None, None, :] - jnp.arange(tq)[None, :, None]
    rel = jnp.where(d[:, None, None] < -1, -S, jnp.where(d[:, None, None] > tq // LANES, S, rel))
    onehot = (_t5_bucket(rel)[None] == jnp.arange(REL_BUCKETS)[:, None, None, None]).astype(F32)
    bias_tab = jnp.einsum("bm,bsqc->msqc", rel_bias.astype(F32), onehot, precision=HIGHEST)
    kern = functools.partial(_attn_a_kernel, tq=tq, S=S, lambda_init=lambda_init)
    return pl.pallas_call(
        kern,
        grid=(B, A_HEADS, nq),
        in_specs=[
            pl.BlockSpec((4, HEAD_DIM), lambda b, h, i: (0, 0)),
            pl.BlockSpec((tq, LANES), lambda b, h, i: (b * nq + i, h)),
            pl.BlockSpec((S, LANES), lambda b, h, i: (b, h)),
            pl.BlockSpec((S, LANES), lambda b, h, i: (b, h)),
            pl.BlockSpec((2, nsl, tq, LANES), lambda b, h, i: (h, 0, 0, 0)),
            pl.BlockSpec((1, LANES), lambda b, h, i: (0, 0)),
        ],
        out_specs=pl.BlockSpec((tq, LANES), lambda b, h, i: (b * nq + i, h)),
        out_shape=jax.ShapeDtypeStruct((B * S, A_HEADS * LANES), BF16),
        compiler_params=_cparams(("parallel", "parallel", "arbitrary")),
        name="l0_attn_diff",
    )(diff_lambda, qa, ka, va, bias_tab, subln_g.reshape(1, LANES))


def _attn_b_kernel(q_ref, k_ref, v_ref, o_ref):
    q = q_ref[...]
    k = k_ref[...]
    v = v_ref[...]
    lane = lax.broadcasted_iota(jnp.int32, (q.shape[0], LANES), 1)
    outs = []
    for u in range(2):
        qp = q[:, u * LANES:(u + 1) * LANES]
        pair = []
        for w in range(2):
            qm = jnp.where((lane < HEAD_DIM) == (w == 0), qp, jnp.zeros_like(qp))
            p = _softmax_rows(_dot_nt(qm, k)).astype(BF16)
            pair.append(_dot(p, v))
        outs.append(jnp.where(lane < HEAD_DIM, pair[0], pair[1]))
    o_ref[...] = jnp.concatenate(outs, axis=1).astype(BF16)


def _attn_b(qb, kb, vb, B, S):
    tq = 256
    nq = S // tq
    return pl.pallas_call(
        _attn_b_kernel,
        grid=(B, B_KV_HEADS, nq),
        in_specs=[
            pl.BlockSpec((tq, 2 * LANES), lambda b, g, i: (b * nq + i, g)),
            pl.BlockSpec((S, LANES), lambda b, g, i: (b, g)),
            pl.BlockSpec((S, LANES), lambda b, g, i: (b, g)),
        ],
        out_specs=pl.BlockSpec((tq, 2 * LANES), lambda b, g, i: (b * nq + i, g)),
        out_shape=jax.ShapeDtypeStruct((B * S, B_Q_HEADS * HEAD_DIM), BF16),
        compiler_params=_cparams(("parallel", "parallel", "arbitrary")),
        name="l0_attn_gqa",
    )(qb, kb, vb)


def _post_kernel(x_ref, ya_ref, yb_ref, wa_ref, wb_ref, mod_ref, g_ref, wq_ref, sk_ref, x1_ref, ht_ref, st_ref):
    y = _dot(ya_ref[...], wa_ref[...]) + _dot(yb_ref[...], wb_ref[...])
    x1 = x_ref[...] + mod_ref[2:3, :] * y
    x1_ref[...] = x1
    h2 = _modnorm(x1, g_ref[...], mod_ref[3:4, :], mod_ref[4:5, :])
    ht_ref[...] = h2.T.astype(BF16)
    q = _dot(h2.astype(BF16), wq_ref[...])
    for hp in range(2 * PEER_HEADS):
        qs = q[:, hp * N_KEYS:(hp + 1) * N_KEYS].astype(BF16)
        st_ref[hp] = _dot_nt(sk_ref[hp], qs)


def _post(x2d, y2d, w_o, mod_l, norm_g, w_q, subkeys, B, S):
    T, D = x2d.shape
    tm = 256
    nper = S // tm
    half = y2d.shape[1] // 2
    nq = w_q.shape[1]
    w_o = w_o.astype(BF16)
    sk = subkeys.reshape(2 * PEER_HEADS, N_KEYS, N_KEYS).astype(BF16)
    const = lambda shp: pl.BlockSpec(shp, lambda i: (0,) * len(shp))
    return pl.pallas_call(
        _post_kernel,
        grid=(T // tm,),
        in_specs=[
            pl.BlockSpec((tm, D), lambda i: (i, 0)),
            pl.BlockSpec((tm, half), lambda i: (i, 0)),
            pl.BlockSpec((tm, half), lambda i: (i, 1)),
            const((half, D)),
            const((half, D)),
            pl.BlockSpec((None, 6, D), lambda i: (i // nper, 0, 0)),
            const((1, D)),
            const((D, nq)),
            const((2 * PEER_HEADS, N_KEYS, N_KEYS)),
        ],
        out_specs=[
            pl.BlockSpec((tm, D), lambda i: (i, 0)),
            pl.BlockSpec((D, tm), lambda i: (0, i)),
            pl.BlockSpec((2 * PEER_HEADS, N_KEYS, tm), lambda i: (0, 0, i)),
        ],
        out_shape=[
            jax.ShapeDtypeStruct((T, D), F32),
            jax.ShapeDtypeStruct((D, T), BF16),
            jax.ShapeDtypeStruct((2 * PEER_HEADS, N_KEYS, T), F32),
        ],
        compiler_params=_cparams(("parallel",)),
        name="mix_out_peer_q",
    )(x2d, y2d, y2d, w_o[:half], w_o[half:], mod_l, norm_g.reshape(1, D), w_q.astype(BF16), sk)


_STAIR = [(i, j) for i in range(PEER_TOPK) for j in range(PEER_TOPK) if (i + 1) * (j + 1) <= PEER_TOPK]
_STAIR_ROWS = 56
_NEG = float("-inf")
WDT = BF16


def _extract_top(s, iota, nsel, sentinel, exact):
    vals = []
    rank = jnp.full(s.shape, float(nsel), F32)
    for r in range(nsel):
        m = jnp.max(s, axis=0, keepdims=True)
        hit = s == m
        if exact:
            hit = iota == jnp.min(jnp.where(hit, iota, sentinel), axis=0, keepdims=True)
        vals.append(m)
        rank = jnp.where(hit, float(r), rank)
        s = jnp.where(hit, _NEG, s)
    return vals, rank


def _peer_head_select(s0, s1, iota, iota_c, group, exact):
    a, rank0 = _extract_top(s0, iota, PEER_TOPK, float(N_KEYS), exact)
    b, rank1 = _extract_top(s1, iota, PEER_TOPK, float(N_KEYS), exact)
    cand = jnp.full(iota_c.shape, _NEG, F32)
    for p, (i, j) in enumerate(_STAIR):
        cand = jnp.where(iota_c == float(p), a[i] + b[j], cand)
    _, crank = _extract_top(cand, iota_c, PEER_TOPK, float(_STAIR_ROWS), exact)
    sel = jnp.where(crank < float(PEER_TOPK), 1.0, 0.0)
    z = jnp.sum(sel * jnp.exp(cand - (a[0] + b[0])), axis=0, keepdims=True)
    n0 = jnp.zeros(s0.shape, F32)
    for i in range(PEER_TOPK):
        cnt = jnp.sum(jnp.where(group == float(i), sel, 0.0), axis=0, keepdims=True)
        n0 = jnp.where(rank0 == float(i), cnt, n0)
    marked = (jnp.sum(jnp.where(rank0 < float(PEER_TOPK), 1.0, 0.0), axis=0, keepdims=True)
              + jnp.sum(jnp.where(rank1 < float(PEER_TOPK), 1.0, 0.0), axis=0, keepdims=True)
              + jnp.sum(sel, axis=0, keepdims=True))
    clean = jnp.max(jnp.abs(marked - 3.0 * PEER_TOPK)) == 0.0
    return rank1, jnp.exp(s1 - b[0]), n0, jnp.exp(s0 - a[0]) * (1.0 / z), clean


def _topk_kernel(st_ref, r1_ref, g1_ref, n0_ref, f0_ref, *, tt):
    iota = lax.broadcasted_iota(jnp.int32, (N_KEYS, tt), 0).astype(F32)
    iota_c = lax.broadcasted_iota(jnp.int32, (_STAIR_ROWS, tt), 0).astype(F32)
    starts = [p for p, (i, j) in enumerate(_STAIR) if j == 0]
    group = jnp.zeros((_STAIR_ROWS, tt), F32)
    for st in starts[1:]:
        group = group + jnp.where(iota_c >= float(st), 1.0, 0.0)

    def store(h, rank1, g1, n0, f0):
        r1_ref[h] = rank1.astype(WDT)
        g1_ref[h] = g1.astype(WDT)
        n0_ref[h] = n0
        f0_ref[h] = f0

    def head_body(h, carry):
        s0 = st_ref[2 * h]
        s1 = st_ref[2 * h + 1]
        rank1, g1, n0, f0, clean = _peer_head_select(s0, s1, iota, iota_c, group, exact=False)
        store(h, rank1, g1, n0, f0)

        @pl.when(jnp.logical_not(clean))
        def _():
            store(h, *_peer_head_select(s0, s1, iota, iota_c, group, exact=True)[:4])

        return carry

    lax.fori_loop(0, PEER_HEADS, head_body, 0)


def _topk(st):
    T = st.shape[-1]
    tt = 256
    blk = pl.BlockSpec((PEER_HEADS, N_KEYS, tt), lambda i: (0, 0, i))
    return pl.pallas_call(
        functools.partial(_topk_kernel, tt=tt),
        grid=(T // tt,),
        in_specs=[pl.BlockSpec((2 * PEER_HEADS, N_KEYS, tt), lambda i: (0, 0, i))],
        out_specs=[blk, blk, blk, blk],
        out_shape=[jax.ShapeDtypeStruct((PEER_HEADS, N_KEYS, T), WDT)] * 2
                  + [jax.ShapeDtypeStruct((PEER_HEADS, N_KEYS, T), F32)] * 2,
        compiler_params=_cparams(("parallel",)),
        name="peer_topk",
    )(st)


_GELU_K0 = -2.0 * math.sqrt(2.0 / math.pi) * math.log2(math.e)
_GELU_K1 = _GELU_K0 * 0.044715


def _gelu_tanh(x):
    return x / (1.0 + jnp.exp2(x * (_GELU_K0 + _GELU_K1 * (x * x))))


def _peer_dense_kernel(x_ref, mod_ref, ht_ref, u_ref, v_ref, r1_ref, g1_ref, n0_ref, f0_ref, o_ref, acc_ref, z_ref, act_ref,
                       *, tb, eb):
    e = pl.program_id(1)

    @pl.when(e == 0)
    def _():
        acc_ref[...] = jnp.zeros_like(acc_ref)

    nch = eb // MXU_DIM
    chunk = lambda k: slice(k * MXU_DIM, (k + 1) * MXU_DIM)
    def activations(k):
        act_ref[k % 2] = _dot(u_ref[chunk(k), :], ht_ref[...])

    activations(0)
    for k in range(nch):
        if k + 1 < nch:
            activations(k + 1)
        for r in range(MXU_DIM // N_KEYS):
            ii = k * (MXU_DIM // N_KEYS) + r
            for c0 in range(0, tb, MXU_DIM):
                cs = slice(c0, c0 + MXU_DIM)
                w = jnp.zeros((N_KEYS, MXU_DIM), WDT)
                for h in range(PEER_HEADS):
                    n0 = n0_ref[h, ii:ii + 1, cs].astype(WDT)
                    f0 = f0_ref[h, ii:ii + 1, cs].astype(WDT)
                    w = w + jnp.where(r1_ref[h, :, cs] < n0, g1_ref[h, :, cs] * f0, jnp.zeros((), WDT))
                act = act_ref[k % 2, r * N_KEYS:(r + 1) * N_KEYS, cs]
                z_ref[ii * N_KEYS:(ii + 1) * N_KEYS, cs] = w * _gelu_tanh(act).astype(WDT)
        acc_ref[...] += lax.dot_general(z_ref[chunk(k), :], v_ref[chunk(k), :], (((0,), (0,)), ((), ())),
                                        preferred_element_type=F32)

    @pl.when(e == pl.num_programs(1) - 1)
    def _():
        o_ref[...] = x_ref[...] + mod_ref[5:6, :] * acc_ref[...]


def _peer_dense(x2d, mod_l, ht, u_tab, v_tab, r1, g1, n0, f0, B, S):
    T, D = x2d.shape
    E = u_tab.shape[0]
    tb = min(512, S)
    eb = 1024
    nper = S // tb
    tok = pl.BlockSpec((PEER_HEADS, N_KEYS, tb), lambda t, e: (0, 0, t))
    rows = pl.BlockSpec((PEER_HEADS, eb // N_KEYS, tb), lambda t, e: (0, e, t))
    return pl.pallas_call(
        functools.partial(_peer_dense_kernel, tb=tb, eb=eb),
        grid=(T // tb, E // eb),
        in_specs=[
            pl.BlockSpec((tb, D), lambda t, e: (t, 0)),
            pl.BlockSpec((None, 6, D), lambda t, e: (t // nper, 0, 0)),
            pl.BlockSpec((D, tb), lambda t, e: (0, t)),
            pl.BlockSpec((eb, D), lambda t, e: (e, 0)),
            pl.BlockSpec((eb, D), lambda t, e: (e, 0)),
            tok, tok, rows, rows,
        ],
        out_specs=pl.BlockSpec((tb, D), lambda t, e: (t, 0)),
        out_shape=jax.ShapeDtypeStruct((T, D), F32),
        scratch_shapes=[pltpu.VMEM((tb, D), F32), pltpu.VMEM((eb, tb), WDT), pltpu.VMEM((2, MXU_DIM, tb), F32)],
        compiler_params=_cparams(("parallel", "arbitrary")),
        name="peer_dense",
    )(x2d, mod_l, ht, u_tab.astype(BF16), v_tab.astype(BF16), r1, g1, n0, f0)


def _rwkv_pre_kernel(x_ref, xp_ref, xn_ref, mod_ref, g_ref, mix_ref, wr_ref, wk_ref, wv_ref, g1_ref, g2_ref,
                     w1_ref, w2_ref, a1_ref, a2_ref, w0_ref, a0_ref, kk_ref, ka_ref, rk_ref, bd_ref,
                     r_o, v_o, g_o, kn_o, lw_o, kz_o, az_o, bon_o, *, nper):
    i = pl.program_id(0)
    gn = g_ref[...]
    sh = mod_ref[0:1, :]
    sc = mod_ref[1:2, :]
    h = _modnorm(x_ref[...], gn, sh, sc)
    tm = h.shape[0]
    hp = _modnorm(xp_ref[...], gn, sh, sc)[7:8, :]
    hn = _modnorm(xn_ref[...], gn, sh, sc)[0:1, :]
    hp = jnp.where(i % nper == 0, 0.0, hp)
    hn = jnp.where(i % nper == nper - 1, 0.0, hn)
    rows = lax.broadcasted_iota(jnp.int32, h.shape, 0)
    prev = jnp.where(rows == 0, hp, pltpu.roll(h, 1, 0))
    nxt = jnp.where(rows == tm - 1, hn, pltpu.roll(h, tm - 1, 0))
    xx = 0.5 * (prev + nxt) - h
    mix = mix_ref[...]
    xr, xw, xk, xv, xa, xg = ((h + xx * mix[j:j + 1, :]).astype(BF16) for j in range(6))
    r = _dot(xr, wr_ref[...])
    k = _dot(xk, wk_ref[...])
    v = _dot(xv, wv_ref[...])
    gl = _dot(xg, g1_ref[...])
    g = _dot((1.0 / (1.0 + jnp.exp(-gl))).astype(BF16), g2_ref[...])
    wl = _dot(jnp.tanh(_dot(xw, w1_ref[...])).astype(BF16), w2_ref[...])
    al = _dot(_dot(xa, a1_ref[...]).astype(BF16), a2_ref[...])
    bd = bd_ref[...]
    kraw = k * kk_ref[...]
    kn = kraw / jnp.maximum(jnp.sqrt(_segsum(kraw * kraw, bd)), 1e-12)
    r_o[...] = r
    v_o[...] = v
    g_o[...] = g
    kn_o[...] = kn
    D = r.shape[1]
    bon = jnp.zeros_like(r)
    for z in range(2):
        u = -(w0_ref[z:z + 1, :] + wl[:, z * D:(z + 1) * D])
        softplus = jnp.maximum(u, 0.0) + jnp.log(1.0 + jnp.exp(-jnp.abs(u)))
        lw_o[z] = -jnp.exp(-softplus - 0.5)
        a = 1.0 / (1.0 + jnp.exp(-(a0_ref[z:z + 1, :] + al[:, z * D:(z + 1) * D])))
        kz = k * (1.0 + (a - 1.0) * ka_ref[...])
        kz_o[z] = kz
        az_o[z] = a
        bon = bon + _segsum(r * kz * rk_ref[...], bd) * v
    bon_o[...] = bon


def _blockdiag2(w):
    n, D = w.shape[1], w.shape[2]
    z = jnp.zeros((n, D), w.dtype)
    return jnp.concatenate([jnp.concatenate([w[0], z], axis=1), jnp.concatenate([z, w[1]], axis=1)], axis=0)


def _rwkv_pre(x2d, mod_l, norm_g, mix, w_rkv, w0, w1, w2, a0, a1, a2, g1, g2, k_k, k_a, r_k, B, S):
    T, D = x2d.shape
    tm = 256
    nper = S // tm
    nblk8 = T // 8
    bd = jnp.kron(jnp.eye(MXU_DIM // RWKV_HEAD, dtype=F32), jnp.ones((RWKV_HEAD, RWKV_HEAD), F32)).astype(BF16)
    w1c = jnp.concatenate([w1[0], w1[1]], axis=1).astype(BF16)
    a1c = jnp.concatenate([a1[0], a1[1]], axis=1).astype(BF16)
    w2b = _blockdiag2(w2).astype(BF16)
    a2b = _blockdiag2(a2).astype(BF16)
    const = lambda shp: pl.BlockSpec(shp, lambda i: (0,) * len(shp), pipeline_mode=pl.Buffered(1))
    row = pl.BlockSpec((tm, D), lambda i: (i, 0))
    row2 = pl.BlockSpec((2, tm, D), lambda i: (0, i, 0))
    return pl.pallas_call(
        functools.partial(_rwkv_pre_kernel, nper=nper),
        grid=(T // tm,),
        in_specs=[
            row,
            pl.BlockSpec((8, D), lambda i: (jnp.maximum(i * (tm // 8) - 1, 0), 0)),
            pl.BlockSpec((8, D), lambda i: (jnp.minimum((i + 1) * (tm // 8), nblk8 - 1), 0)),
            pl.BlockSpec((None, 6, D), lambda i: (i // nper, 0, 0)),
            const((1, D)), const((6, D)),
            const((D, D)), const((D, D)), const((D, D)),
            const(g1.shape), const(g2.shape),
            const(w1c.shape), const(w2b.shape), const(a1c.shape), const(a2b.shape),
            const((2, D)), const((2, D)), const((1, D)), const((1, D)), const((1, D)),
            const((MXU_DIM, MXU_DIM)),
        ],
        out_specs=[row, row, row, row, row2, row2, row2, row],
        out_shape=[jax.ShapeDtypeStruct((T, D), F32)] * 4 + [jax.ShapeDtypeStruct((2, T, D), F32)] * 3
                  + [jax.ShapeDtypeStruct((T, D), F32)],
        compiler_params=_cparams(("parallel",)),
        name="l1_rwkv_proj",
    )(x2d, x2d, x2d, mod_l, norm_g.reshape(1, D), mix,
      w_rkv[0].astype(BF16), w_rkv[1].astype(BF16), w_rkv[2].astype(BF16), g1.astype(BF16), g2.astype(BF16),
      w1c, w2b, a1c, a2b, w0, a0, k_k.reshape(1, D), k_a.reshape(1, D), r_k.reshape(1, D), bd)


SCAN_C = 64
SCAN_W = 256
SCAN_HEADS = SCAN_W // RWKV_HEAD


def _scan_chunk(refs, cs, s_ref, reverse):
    r, v, kn, lw, kz, az = (ref[:, cs] for ref in refs)
    S = s_ref[...]
    C = SCAN_C
    n = SCAN_HEADS * C
    ti = lax.broadcasted_iota(jnp.int32, (C, C), 0)
    si = lax.broadcasted_iota(jnp.int32, (C, C), 1)
    inc = jnp.where((si >= ti) if reverse else (si <= ti), 1.0, 0.0).astype(BF16)
    lw_hi = lw.astype(BF16)
    lw_lo = (lw - lw_hi.astype(F32)).astype(BF16)
    cum = _dot(inc, lw_hi) + _dot(inc, lw_lo)
    yield
    tot = cum[0:1, :] if reverse else cum[C - 1:C, :]
    a = -kn
    b = kn * az
    pinv = jnp.exp(-cum)
    pend = jnp.exp(tot - cum)
    rt = r * jnp.exp(cum)
    at = a * jnp.exp(cum - lw)
    bt = (b * pinv).astype(BF16)
    kt = (kz * pinv).astype(BF16)

    lane = lax.broadcasted_iota(jnp.int32, (C, SCAN_W), 1) // RWKV_HEAD
    stack = lambda t: jnp.concatenate([jnp.where(lane == hh, t, 0.0) for hh in range(SCAN_HEADS)], axis=0)
    tile = lambda t: jnp.concatenate([t] * SCAN_HEADS, axis=0)
    ri = lax.broadcasted_iota(jnp.int32, (n, n), 0)
    ci = lax.broadcasted_iota(jnp.int32, (n, n), 1)
    same = (ri // C) == (ci // C)
    tt = ri % C
    ss = ci % C
    m_str = same & ((ss > tt) if reverse else (ss < tt))
    m_inc = same & ((ss >= tt) if reverse else (ss <= tt))

    at_s = stack(at).astype(BF16)
    rt_s = stack(rt).astype(BF16)
    bt_t = tile(bt)
    kt_t = tile(kt)
    a_ab = jnp.where(m_str, _dot_nt(at_s, bt_t), 0.0).astype(BF16)
    yield
    a_ak = jnp.where(m_str, _dot_nt(at_s, kt_t), 0.0).astype(BF16)
    yield
    a_rb = jnp.where(m_inc, _dot_nt(rt_s, bt_t), 0.0).astype(BF16)
    yield
    a_rk = jnp.where(m_inc, _dot_nt(rt_s, kt_t), 0.0).astype(BF16)
    yield
    x0 = _dot_nt(jnp.concatenate([at, rt], axis=0).astype(BF16), S.astype(BF16))
    yield
    a_s0 = x0[:C]
    r_s0 = x0[C:]
    v_t = tile(v).astype(BF16)
    x = tile(a_s0) + _dot(a_ak, v_t)
    yield
    ap = a_ab
    nsteps = int(math.log2(C))
    for it in range(nsteps):
        x = x + _dot(ap, x.astype(BF16))
        yield
        if it + 1 < nsteps:
            ap = _dot(ap, ap).astype(BF16)
            yield
    y_s = _dot(a_rb, x.astype(BF16))
    yield
    y_s = y_s + _dot(a_rk, v_t)
    yield
    unstack = lambda t: sum(jnp.where(lane == hh, t[hh * C:(hh + 1) * C], 0.0) for hh in range(SCAN_HEADS))
    u = unstack(x)
    y = r_s0 + unstack(y_s)
    uv = jnp.concatenate([u, v], axis=0)
    bk = jnp.concatenate([b * pend, kz * pend], axis=0)
    upd = _dot(uv.T.astype(BF16), bk.astype(BF16))
    yield
    vi = lax.broadcasted_iota(jnp.int32, (SCAN_W, SCAN_W), 0) // RWKV_HEAD
    ki = lax.broadcasted_iota(jnp.int32, (SCAN_W, SCAN_W), 1) // RWKV_HEAD
    s_ref[...] = S * jnp.exp(tot) + jnp.where(vi == ki, upd, 0.0)
    return y


SCAN_GROUPS = 4


def _scan_kernel(rf, vf, knf, lwf, kzf, azf, rb, vb, knb, lwb, kzb, azb, yf_o, yb_o, s_ref):
    @pl.when(pl.program_id(2) == 0)
    def _():
        s_ref[...] = jnp.zeros_like(s_ref)

    chains = []
    for g in range(SCAN_GROUPS):
        cs = slice(g * SCAN_W, (g + 1) * SCAN_W)
        chains.append((_scan_chunk((rf, vf, knf, lwf, kzf, azf), cs, s_ref.at[0, g], False), yf_o, cs))
        chains.append((_scan_chunk((rb, vb, knb, lwb, kzb, azb), cs, s_ref.at[1, g], True), yb_o, cs))
    while chains:
        for chain in list(chains):
            gen, out_ref, cs = chain
            try:
                next(gen)
            except StopIteration as done:
                out_ref[:, cs] = done.value
                chains.remove(chain)


def _rwkv_scan(r, v, kn, lw, kz, az, B, S):
    T, D = r.shape
    C = SCAN_C
    nc = S // C
    W = SCAN_W * SCAN_GROUPS
    ng = D // W
    fwd = lambda b, g, c: (b * nc + c, g)
    bwd = lambda b, g, c: (b * nc + nc - 1 - c, g)
    fwd3 = lambda z: (lambda b, g, c: (z, b * nc + c, g))
    bwd3 = lambda z: (lambda b, g, c: (z, b * nc + nc - 1 - c, g))
    s2 = lambda im: pl.BlockSpec((C, W), im)
    s3 = lambda im: pl.BlockSpec((None, C, W), im)
    return pl.pallas_call(
        _scan_kernel,
        grid=(B, ng, nc),
        in_specs=[s2(fwd), s2(fwd), s2(fwd), s3(fwd3(0)), s3(fwd3(0)), s3(fwd3(0)),
                  s2(bwd), s2(bwd), s2(bwd), s3(bwd3(1)), s3(bwd3(1)), s3(bwd3(1))],
        out_specs=[s2(fwd), s2(bwd)],
        out_shape=[jax.ShapeDtypeStruct((T, D), F32)] * 2,
        scratch_shapes=[pltpu.VMEM((2, SCAN_GROUPS, SCAN_W, SCAN_W), F32)],
        compiler_params=_cparams(("parallel", "parallel", "arbitrary")),
        name="l1_wkv_scan",
    )(r, v, kn, lw, kz, az, r, v, kn, lw, kz, az)


def _rwkv_out_kernel(yf_ref, yb_ref, bon_ref, g_ref, lg_ref, lb_ref, bd_ref, o_ref):
    y = yf_ref[...] + yb_ref[...]
    bd = bd_ref[...]
    mu = _segsum(y, bd) * (1.0 / RWKV_HEAD)
    yc = y - mu
    var = _segsum(yc * yc, bd) * (1.0 / RWKV_HEAD)
    yn = yc * lax.rsqrt(var + GN_EPS) * lg_ref[...] + lb_ref[...]
    o_ref[...] = ((yn + bon_ref[...]) * g_ref[...]).astype(BF16)


def _rwkv_out(yf, yb, bon, g, ln_g, ln_b):
    T, D = yf.shape
    tm = 256
    bd = jnp.kron(jnp.eye(MXU_DIM // RWKV_HEAD, dtype=F32), jnp.ones((RWKV_HEAD, RWKV_HEAD), F32)).astype(BF16)
    row = pl.BlockSpec((tm, D), lambda i: (i, 0))
    const = lambda shp: pl.BlockSpec(shp, lambda i: (0,) * len(shp))
    return pl.pallas_call(
        _rwkv_out_kernel,
        grid=(T // tm,),
        in_specs=[row, row, row, row, const((1, D)), const((1, D)), const((MXU_DIM, MXU_DIM))],
        out_specs=row,
        out_shape=jax.ShapeDtypeStruct((T, D), BF16),
        compiler_params=_cparams(("parallel",)),
        name="l1_rwkv_norm_gate",
    )(yf, yb, bon, g, ln_g.reshape(1, D), ln_b.reshape(1, D), bd)


def _peer_layer(x2d, y2d, w_o, mod_l, norm_g2, w_q, subkeys, u_tab, v_tab, B, S):
    x1, ht, st = _post(x2d, y2d, w_o, mod_l, norm_g2, w_q, subkeys, B, S)
    r1, g1, n0, f0 = _topk(st)
    return _peer_dense(x1, mod_l, ht, u_tab, v_tab, r1, g1, n0, f0, B, S)


def kernel(x, c, ada_w, ada_b, norm_g, attn_w_in, attn_qk_g, diff_lambda, diff_subln_g, attn_w_out, rel_bias,
           rwkv_mix, rwkv_w_rkv, rwkv_w0, rwkv_w1, rwkv_w2, rwkv_a0, rwkv_a1, rwkv_a2, rwkv_g1, rwkv_g2,
           rwkv_k_k, rwkv_k_a, rwkv_r_k, rwkv_ln_g, rwkv_ln_b, rwkv_w_o,
           peer_w_q, peer_subkeys, peer_u, peer_v):
    B, S, D = x.shape
    mod = _ada(c, ada_w, ada_b)
    x2d = x.reshape(B * S, D)
    for layer in range(DEPTH):
        j = layer // 2
        mod_l = mod[layer]
        if layer % 2 == 0:
            lambda_init = 0.8 - 0.6 * math.exp(-0.3 * layer)
            qa, ka, va, qb, kb, vb = _inproj(x2d, mod_l, norm_g[layer, 0], attn_w_in[j], attn_qk_g[j], B, S)
            oa = _attn_a(qa, ka, va, diff_lambda[j], diff_subln_g[j], rel_bias, lambda_init, B, S)
            ob = _attn_b(qb, kb, vb, B, S)
            y2d = jnp.concatenate([oa, ob], axis=1)
            w_o = attn_w_out[j]
        else:
            r, v, g, kn, lw, kz, az, bon = _rwkv_pre(
                x2d, mod_l, norm_g[layer, 0], rwkv_mix[j], rwkv_w_rkv[j], rwkv_w0[j], rwkv_w1[j], rwkv_w2[j],
                rwkv_a0[j], rwkv_a1[j], rwkv_a2[j], rwkv_g1[j], rwkv_g2[j], rwkv_k_k[j], rwkv_k_a[j], rwkv_r_k[j], B, S)
            yf, yb = _rwkv_scan(r, v, kn, lw, kz, az, B, S)
            y2d = _rwkv_out(yf, yb, bon, g, rwkv_ln_g[j], rwkv_ln_b[j])
            w_o = rwkv_w_o[j]
        x2d = _peer_layer(x2d, y2d, w_o, mod_l, norm_g[layer, 1], peer_w_q[layer], peer_subkeys[layer],
                          peer_u[layer], peer_v[layer], B, S)
    return x2d.reshape(B, S, D)
```

```python
import functools
import math

import jax
import jax.numpy as jnp
from jax import lax
from jax.experimental import pallas as pl
from jax.experimental.pallas import tpu as pltpu

F32 = jnp.float32
BF16 = jnp.bfloat16
HIGHEST = lax.Precision.HIGHEST

D_MODEL = 1024
DEPTH = 2
HEAD_DIM = 64
A_HEADS = 4
A_MAPS = 8
B_Q_HEADS = 8
B_KV_HEADS = 2
GRID_W = 64
ROPE_AXIS_DIM = 32
ROPE_THETA = 10000.0
REL_BUCKETS = 32
REL_MAX_DIST = 128
RWKV_HEAD = 64
GN_EPS = 64e-5
PEER_HEADS = 8
N_KEYS = 128
PEER_TOPK = 16
NORM_EPS = 1e-6

LANES = 128
MXU_DIM = 256
VMEM_LIMIT = 56 * 1024 * 1024

NT_DIMS = (((1,), (1,)), ((), ()))


def _cparams(sem):
    return pltpu.CompilerParams(dimension_semantics=sem, vmem_limit_bytes=VMEM_LIMIT)


def _dot(a, b):
    return jnp.dot(a, b, preferred_element_type=F32)


def _dot_nt(a, b):
    return lax.dot_general(a, b, NT_DIMS, preferred_element_type=F32)


def _segsum(v, bd):
    outs = []
    n = v.shape[1]
    for c0 in range(0, n, MXU_DIM):
        w = min(MXU_DIM, n - c0)
        blk = v[:, c0:c0 + w]
        hi = blk.astype(BF16)
        lo = (blk - hi.astype(F32)).astype(BF16)
        b = bd[:w, :w]
        outs.append(_dot(hi, b) + _dot(lo, b))
    return outs[0] if len(outs) == 1 else jnp.concatenate(outs, axis=1)


def _modnorm(x, g, sh, sc):
    ms = jnp.mean(x * x, axis=-1, keepdims=True)
    return (x * lax.rsqrt(ms + NORM_EPS) * g) * (1.0 + sc) + sh


def _ada_kernel(c_ref, w_ref, b_ref, o_ref):
    c = c_ref[...]
    cs = c * (1.0 / (1.0 + jnp.exp(-c)))
    o_ref[...] = jnp.dot(cs, w_ref[...], preferred_element_type=F32, precision=HIGHEST) + b_ref[...]


def _ada(c, ada_w, ada_b):
    B, D = c.shape
    n6 = ada_w.shape[-1]
    tn = 1536
    out = pl.pallas_call(
        _ada_kernel,
        grid=(DEPTH, n6 // tn),
        in_specs=[
            pl.BlockSpec((B, D), lambda l, j: (0, 0)),
            pl.BlockSpec((None, D, tn), lambda l, j: (l, 0, j)),
            pl.BlockSpec((None, 1, tn), lambda l, j: (l, 0, j)),
        ],
        out_specs=pl.BlockSpec((None, B, tn), lambda l, j: (l, 0, j)),
        out_shape=jax.ShapeDtypeStruct((DEPTH, B, n6), F32),
        compiler_params=_cparams(("parallel", "parallel")),
        name="ada_mod",
    )(c, ada_w, ada_b.reshape(DEPTH, 1, n6))
    return out.reshape(DEPTH, B, 6, D)


C_QA, C_KA, C_VA, C_QB, C_KB, C_VB, C_QBS, C_KBS, C_END = 0, 512, 1024, 1536, 2048, 2304, 2560, 3072, 3328


def _inproj_kernel(x_ref, mod_ref, g_ref, w_ref, gain_ref, cos_ref, sin_ref, bd_ref,
                   qa_ref, ka_ref, va_ref, qb_ref, kb_ref, vb_ref):
    h = _modnorm(x_ref[...], g_ref[...], mod_ref[0:1, :], mod_ref[1:2, :])
    y = _dot(h.astype(BF16), w_ref[...])
    bd = bd_ref[...]
    gain = gain_ref[...]

    def rinv(sec):
        return lax.rsqrt(_segsum(sec * sec, bd) * (1.0 / HEAD_DIM) + NORM_EPS)

    yqa = y[:, C_QA:C_KA]
    qa_ref[...] = (yqa * rinv(yqa) * gain[:, C_QA:C_KA]).astype(BF16)
    yka = y[:, C_KA:C_VA]
    ka_ref[...] = (yka * rinv(yka) * gain[:, C_KA:C_VA]).astype(BF16)
    va_ref[...] = y[:, C_VA:C_QB].astype(BF16)
    cos = cos_ref[...]
    sin = sin_ref[...]
    yqb = y[:, C_QB:C_KB]
    qb = (yqb * gain[:, C_QB:C_KB] * cos + y[:, C_QBS:C_KBS] * gain[:, C_QBS:C_KBS] * sin) * rinv(yqb)
    qb_ref[...] = qb.astype(BF16)
    ykb = y[:, C_KB:C_VB]
    kb = (ykb * gain[:, C_KB:C_VB] * cos[:, :256] + y[:, C_KBS:C_END] * gain[:, C_KBS:C_END] * sin[:, :256]) * rinv(ykb)
    kb_ref[...] = kb.astype(BF16)
    vb_ref[...] = y[:, C_VB:C_QBS].astype(BF16)


def _swap16(a):
    sh = a.shape
    a = a.reshape(sh[:-1] + (sh[-1] // 32, 2, 16))
    return a[..., ::-1, :].reshape(sh)


def _inproj(x2d, mod_l, norm_g, w_in, qk_g, B, S):
    T, D = x2d.shape
    tm = 256
    nper = S // tm
    scale = HEAD_DIM ** -0.5
    wqa, wka, wva, wqb, wkb, wvb = (w_in[:, a:b] for a, b in
                                    ((0, 512), (512, 1024), (1024, 1536), (1536, 2048), (2048, 2176), (2176, 2304)))
    dup = lambda w: jnp.repeat(w.reshape(D, B_KV_HEADS, 1, HEAD_DIM), 2, axis=2).reshape(D, 4 * HEAD_DIM)
    wkbd = dup(wkb)
    w_all = jnp.concatenate([wqa, wka, wva, wqb, wkbd, dup(wvb), _swap16(wqb), _swap16(wkbd)], axis=1).astype(BF16)
    gq = jnp.tile(qk_g[2], B_Q_HEADS) * scale
    gk = jnp.tile(qk_g[3], 4)
    gain = jnp.concatenate([jnp.tile(qk_g[0], A_MAPS) * scale, jnp.tile(qk_g[1], A_MAPS), jnp.ones((512,), F32),
                            gq, gk, jnp.ones((256,), F32), _swap16(gq), _swap16(gk)]).reshape(1, C_END)
    pos = jnp.arange(S)
    row = (pos // GRID_W).astype(F32)
    col = (pos % GRID_W).astype(F32)
    inv = ROPE_THETA ** (-jnp.arange(0, ROPE_AXIS_DIM, 2, dtype=F32) / ROPE_AXIS_DIM)
    ang = jnp.concatenate([row[:, None] * inv] * 2 + [col[:, None] * inv] * 2, axis=1)
    sign = jnp.tile(jnp.concatenate([-jnp.ones((16,), F32), jnp.ones((16,), F32)]), 2)
    cos_t = jnp.tile(jnp.cos(ang), (1, B_Q_HEADS))
    sin_t = jnp.tile(jnp.sin(ang) * sign, (1, B_Q_HEADS))
    bd = jnp.kron(jnp.eye(MXU_DIM // HEAD_DIM, dtype=F32), jnp.ones((HEAD_DIM, HEAD_DIM), F32)).astype(BF16)
    row_spec = lambda w: pl.BlockSpec((tm, w), lambda i: (i, 0))
    const = lambda shp: pl.BlockSpec(shp, lambda i: (0,) * len(shp))
    outs = pl.pallas_call(
        _inproj_kernel,
        grid=(T // tm,),
        in_specs=[
            row_spec(D),
            pl.BlockSpec((None, 6, D), lambda i: (i // nper, 0, 0)),
            const((1, D)),
            const((D, C_END)),
            const((1, C_END)),
            pl.BlockSpec((tm, 512), lambda i: (i % nper, 0)),
            pl.BlockSpec((tm, 512), lambda i: (i % nper, 0)),
            const((MXU_DIM, MXU_DIM)),
        ],
        out_specs=[row_spec(512), row_spec(512), row_spec(512), row_spec(512), row_spec(256), row_spec(256)],
        out_shape=[jax.ShapeDtypeStruct((T, w), BF16) for w in (512, 512, 512, 512, 256, 256)],
        compiler_params=_cparams(("parallel",)),
        name="l0_inproj",
    )(x2d, mod_l, norm_g.reshape(1, D), w_all, gain, cos_t, sin_t, bd)
    return outs


def _t5_bucket(rel):
    nb = REL_BUCKETS // 2
    max_exact = nb // 2
    ret = jnp.where(rel > 0, nb, 0)
    n = jnp.abs(rel)
    nf = jnp.maximum(n, 1).astype(F32)
    large = max_exact + (jnp.log(nf / max_exact) / math.log(REL_MAX_DIST / max_exact) * (nb - max_exact)).astype(jnp.int32)
    large = jnp.minimum(large, nb - 1)
    return ret + jnp.where(n < max_exact, n, large)


def _softmax_rows(s):
    m = jnp.max(s, axis=1, keepdims=True)
    e = jnp.exp(s - m)
    l = jnp.sum(e, axis=1, keepdims=True)
    return e * (1.0 / l)


def _attn_a_kernel(lam_ref, q_ref, k_ref, v_ref, bias_ref, sg_ref, o_ref, *, tq, nsub, S, lambda_init):
    i = pl.program_id(2)
    lv = lam_ref[...]
    lam = (jnp.exp(jnp.sum(lv[0:1] * lv[1:2], axis=1, keepdims=True))
           - jnp.exp(jnp.sum(lv[2:3] * lv[3:4], axis=1, keepdims=True)) + lambda_init)
    k = k_ref[...]
    v = v_ref[...]
    lane = lax.broadcasted_iota(jnp.int32, (tq, LANES), 1)
    nsl = tq // LANES + 4

    def scores(sub):
        q = q_ref[sub * tq:(sub + 1) * tq, :]
        return [_dot_nt(jnp.where((lane < HEAD_DIM) == (mi == 0), q, jnp.zeros_like(q)), k) for mi in range(2)]

    s_next = scores(0)
    for sub in range(nsub):
        ss = s_next
        if sub + 1 < nsub:
            s_next = scores(sub + 1)
        blk = i * nsub + sub
        ps = []
        for mi in range(2):
            tiles = []
            for j in range(S // LANES):
                idx = jnp.clip(j - blk * (tq // LANES) + 2, 0, nsl - 1)
                tiles.append(bias_ref[mi, idx])
            ps.append(_softmax_rows(ss[mi] + jnp.concatenate(tiles, axis=1)))
        diff = (ps[0] - lam * ps[1]).astype(BF16)
        oa = _dot(diff, v)
        ms = jnp.mean(oa * oa, axis=1, keepdims=True)
        o_ref[sub * tq:(sub + 1) * tq, :] = (oa * lax.rsqrt(ms + NORM_EPS) * sg_ref[...]
                                             * (1.0 - lambda_init)).astype(BF16)


def _attn_a(qa, ka, va, diff_lambda, subln_g, rel_bias, lambda_init, B, S):
    tq = 256
    nsub = 2 if S % (2 * tq) == 0 else 1
    tqb = tq * nsub
    nq = S // tqb
    nsl = tq // LANES + 4
    d = jnp.arange(nsl) - 2
    rel = d[:, None, None] * LANES + jnp.arange(LANES)[None, None, :] - jnp.arange(tq)[None, :, None]
    rel = jnp.where(d[:, None, None] < -1, -S, jnp.where(d[:, None, None] > tq // LANES, S, rel))
    onehot = (_t5_bucket(rel)[None] == jnp.arange(REL_BUCKETS)[:, None, None, None]).astype(F32)
    bias_tab = jnp.einsum("bm,bsqc->msqc", rel_bias.astype(F32), onehot, precision=HIGHEST)
    kern = functools.partial(_attn_a_kernel, tq=tq, nsub=nsub, S=S, lambda_init=lambda_init)
    return pl.pallas_call(
        kern,
        grid=(B, A_HEADS, nq),
        in_specs=[
            pl.BlockSpec((4, HEAD_DIM), lambda b, h, i: (0, 0)),
            pl.BlockSpec((tqb, LANES), lambda b, h, i: (b * nq + i, h)),
            pl.BlockSpec((S, LANES), lambda b, h, i: (b, h)),
            pl.BlockSpec((S, LANES), lambda b, h, i: (b, h)),
            pl.BlockSpec((2, nsl, tq, LANES), lambda b, h, i: (h, 0, 0, 0)),
            pl.BlockSpec((1, LANES), lambda b, h, i: (0, 0)),
        ],
        out_specs=pl.BlockSpec((tqb, LANES), lambda b, h, i: (b * nq + i, h)),
        out_shape=jax.ShapeDtypeStruct((B * S, A_HEADS * LANES), BF16),
        compiler_params=_cparams(("parallel", "parallel", "arbitrary")),
        name="l0_attn_diff",
    )(diff_lambda, qa, ka, va, bias_tab, subln_g.reshape(1, LANES))


def _attn_b_kernel(q_ref, k_ref, v_ref, o_ref):
    q = q_ref[...]
    k = k_ref[...]
    v = v_ref[...]
    lane = lax.broadcasted_iota(jnp.int32, (q.shape[0], LANES), 1)

    def scores(r):
        qp = q[:, (r // 2) * LANES:(r // 2 + 1) * LANES]
        qm = jnp.where((lane < HEAD_DIM) == (r % 2 == 0), qp, jnp.zeros_like(qp))
        return _dot_nt(qm, k)

    nh = B_Q_HEADS // B_KV_HEADS
    heads = []
    s_next = scores(0)
    for r in range(nh):
        s = s_next
        if r + 1 < nh:
            s_next = scores(r + 1)
        heads.append(_dot(_softmax_rows(s).astype(BF16), v))
    outs = [jnp.where(lane < HEAD_DIM, heads[2 * u], heads[2 * u + 1]) for u in range(nh // 2)]
    o_ref[...] = jnp.concatenate(outs, axis=1).astype(BF16)


def _attn_b(qb, kb, vb, B, S):
    tq = 256
    nq = S // tq
    return pl.pallas_call(
        _attn_b_kernel,
        grid=(B, B_KV_HEADS, nq),
        in_specs=[
            pl.BlockSpec((tq, 2 * LANES), lambda b, g, i: (b * nq + i, g)),
            pl.BlockSpec((S, LANES), lambda b, g, i: (b, g)),
            pl.BlockSpec((S, LANES), lambda b, g, i: (b, g)),
        ],
        out_specs=pl.BlockSpec((tq, 2 * LANES), lambda b, g, i: (b * nq + i, g)),
        out_shape=jax.ShapeDtypeStruct((B * S, B_Q_HEADS * HEAD_DIM), BF16),
        compiler_params=_cparams(("parallel", "parallel", "arbitrary")),
        name="l0_attn_gqa",
    )(qb, kb, vb)


def _post_kernel(x_ref, ya_ref, yb_ref, wa_ref, wb_ref, mod_ref, g_ref, wq_ref, sk_ref, x1_ref, ht_ref, st_ref):
    y = _dot(ya_ref[...], wa_ref[...]) + _dot(yb_ref[...], wb_ref[...])
    x1 = x_ref[...] + mod_ref[2:3, :] * y
    x1_ref[...] = x1
    h2 = _modnorm(x1, g_ref[...], mod_ref[3:4, :], mod_ref[4:5, :])
    ht_ref[...] = h2.T.astype(BF16)
    q = _dot(h2.astype(BF16), wq_ref[...])
    for hp in range(2 * PEER_HEADS):
        qs = q[:, hp * N_KEYS:(hp + 1) * N_KEYS].astype(BF16)
        st_ref[hp] = _dot_nt(sk_ref[hp], qs)


def _post(x2d, y2d, w_o, mod_l, norm_g, w_q, subkeys, B, S):
    T, D = x2d.shape
    tm = 256
    nper = S // tm
    half = y2d.shape[1] // 2
    nq = w_q.shape[1]
    w_o = w_o.astype(BF16)
    sk = subkeys.reshape(2 * PEER_HEADS, N_KEYS, N_KEYS).astype(BF16)
    const = lambda shp: pl.BlockSpec(shp, lambda i: (0,) * len(shp))
    return pl.pallas_call(
        _post_kernel,
        grid=(T // tm,),
        in_specs=[
            pl.BlockSpec((tm, D), lambda i: (i, 0)),
            pl.BlockSpec((tm, half), lambda i: (i, 0)),
            pl.BlockSpec((tm, half), lambda i: (i, 1)),
            const((half, D)),
            const((half, D)),
            pl.BlockSpec((None, 6, D), lambda i: (i // nper, 0, 0)),
            const((1, D)),
            const((D, nq)),
            const((2 * PEER_HEADS, N_KEYS, N_KEYS)),
        ],
        out_specs=[
            pl.BlockSpec((tm, D), lambda i: (i, 0)),
            pl.BlockSpec((D, tm), lambda i: (0, i)),
            pl.BlockSpec((2 * PEER_HEADS, N_KEYS, tm), lambda i: (0, 0, i)),
        ],
        out_shape=[
            jax.ShapeDtypeStruct((T, D), F32),
            jax.ShapeDtypeStruct((D, T), BF16),
            jax.ShapeDtypeStruct((2 * PEER_HEADS, N_KEYS, T), F32),
        ],
        compiler_params=_cparams(("parallel",)),
        name="mix_out_peer_q",
    )(x2d, y2d, y2d, w_o[:half], w_o[half:], mod_l, norm_g.reshape(1, D), w_q.astype(BF16), sk)


_STAIR = [(i, j) for i in range(PEER_TOPK) for j in range(PEER_TOPK) if (i + 1) * (j + 1) <= PEER_TOPK]
_STAIR_ROWS = 56
_NEG = float("-inf")
WDT = BF16


def _extract_top(s, iota, nsel, sentinel, exact):
    vals = []
    rank = jnp.full(s.shape, float(nsel), F32)
    for r in range(nsel):
        m = jnp.max(s, axis=0, keepdims=True)
        hit = s == m
        if exact:
            hit = iota == jnp.min(jnp.where(hit, iota, sentinel), axis=0, keepdims=True)
        vals.append(m)
        rank = jnp.where(hit, float(r), rank)
        s = jnp.where(hit, _NEG, s)
    return vals, rank


def _peer_head_select(s0, s1, iota, iota_c, group, exact):
    a, rank0 = _extract_top(s0, iota, PEER_TOPK, float(N_KEYS), exact)
    b, rank1 = _extract_top(s1, iota, PEER_TOPK, float(N_KEYS), exact)
    cand = jnp.full(iota_c.shape, _NEG, F32)
    for p, (i, j) in enumerate(_STAIR):
        cand = jnp.where(iota_c == float(p), a[i] + b[j], cand)
    _, crank = _extract_top(cand, iota_c, PEER_TOPK, float(_STAIR_ROWS), exact)
    sel = jnp.where(crank < float(PEER_TOPK), 1.0, 0.0)
    z = jnp.sum(sel * jnp.exp(cand - (a[0] + b[0])), axis=0, keepdims=True)
    n0 = jnp.zeros(s0.shape, F32)
    for i in range(PEER_TOPK):
        cnt = jnp.sum(jnp.where(group == float(i), sel, 0.0), axis=0, keepdims=True)
        n0 = jnp.where(rank0 == float(i), cnt, n0)
    marked = (jnp.sum(jnp.where(rank0 < float(PEER_TOPK), 1.0, 0.0), axis=0, keepdims=True)
              + jnp.sum(jnp.where(rank1 < float(PEER_TOPK), 1.0, 0.0), axis=0, keepdims=True)
              + jnp.sum(sel, axis=0, keepdims=True))
    clean = jnp.max(jnp.abs(marked - 3.0 * PEER_TOPK)) == 0.0
    return rank1, jnp.exp(s1 - b[0]), n0, jnp.exp(s0 - a[0]) * (1.0 / z), clean


def _topk_kernel(st_ref, r1_ref, g1_ref, n0_ref, f0_ref, *, tt):
    iota = lax.broadcasted_iota(jnp.int32, (N_KEYS, tt), 0).astype(F32)
    iota_c = lax.broadcasted_iota(jnp.int32, (_STAIR_ROWS, tt), 0).astype(F32)
    starts = [p for p, (i, j) in enumerate(_STAIR) if j == 0]
    group = jnp.zeros((_STAIR_ROWS, tt), F32)
    for st in starts[1:]:
        group = group + jnp.where(iota_c >= float(st), 1.0, 0.0)

    def store(h, rank1, g1, n0, f0):
        r1_ref[h] = rank1.astype(WDT)
        g1_ref[h] = g1.astype(WDT)
        n0_ref[h] = n0
        f0_ref[h] = f0

    def head_body(h, carry):
        s0 = st_ref[2 * h]
        s1 = st_ref[2 * h + 1]
        rank1, g1, n0, f0, clean = _peer_head_select(s0, s1, iota, iota_c, group, exact=False)
        store(h, rank1, g1, n0, f0)

        @pl.when(jnp.logical_not(clean))
        def _():
            store(h, *_peer_head_select(s0, s1, iota, iota_c, group, exact=True)[:4])

        return carry

    lax.fori_loop(0, PEER_HEADS, head_body, 0)


def _topk(st):
    T = st.shape[-1]
    tt = 256
    blk = pl.BlockSpec((PEER_HEADS, N_KEYS, tt), lambda i: (0, 0, i))
    return pl.pallas_call(
        functools.partial(_topk_kernel, tt=tt),
        grid=(T // tt,),
        in_specs=[pl.BlockSpec((2 * PEER_HEADS, N_KEYS, tt), lambda i: (0, 0, i))],
        out_specs=[blk, blk, blk, blk],
        out_shape=[jax.ShapeDtypeStruct((PEER_HEADS, N_KEYS, T), WDT)] * 2
                  + [jax.ShapeDtypeStruct((PEER_HEADS, N_KEYS, T), F32)] * 2,
        compiler_params=_cparams(("parallel",)),
        name="peer_topk",
    )(st)


_GELU_K0 = -2.0 * math.sqrt(2.0 / math.pi) * math.log2(math.e)
_GELU_K1 = _GELU_K0 * 0.044715


def _gelu_tanh(x):
    return x / (1.0 + jnp.exp2(x * (_GELU_K0 + _GELU_K1 * (x * x))))


def _peer_dense_kernel(x_ref, mod_ref, ht_ref, u_ref, v_ref, r1_ref, g1_ref, n0_ref, f0_ref, o_ref, acc_ref, z_ref, act_ref,
                       *, tb, eb):
    e = pl.program_id(1)

    @pl.when(e == 0)
    def _():
        acc_ref[...] = jnp.zeros_like(acc_ref)

    nch = eb // MXU_DIM
    chunk = lambda k: slice(k * MXU_DIM, (k + 1) * MXU_DIM)
    def activations(k):
        act_ref[k % 2] = _dot(u_ref[chunk(k), :], ht_ref[...])

    activations(0)
    for k in range(nch):
        if k + 1 < nch:
            activations(k + 1)
        for r in range(MXU_DIM // N_KEYS):
            ii = k * (MXU_DIM // N_KEYS) + r
            for c0 in range(0, tb, MXU_DIM):
                cs = slice(c0, c0 + MXU_DIM)
                w = jnp.zeros((N_KEYS, MXU_DIM), WDT)
                for h in range(PEER_HEADS):
                    n0 = n0_ref[h, ii:ii + 1, cs].astype(WDT)
                    f0 = f0_ref[h, ii:ii + 1, cs].astype(WDT)
                    w = w + jnp.where(r1_ref[h, :, cs] < n0, g1_ref[h, :, cs] * f0, jnp.zeros((), WDT))
                act = act_ref[k % 2, r * N_KEYS:(r + 1) * N_KEYS, cs]
                z_ref[ii * N_KEYS:(ii + 1) * N_KEYS, cs] = w * _gelu_tanh(act).astype(WDT)
        acc_ref[...] += lax.dot_general(z_ref[chunk(k), :], v_ref[chunk(k), :], (((0,), (0,)), ((), ())),
                                        preferred_element_type=F32)

    @pl.when(e == pl.num_programs(1) - 1)
    def _():
        o_ref[...] = x_ref[...] + mod_ref[5:6, :] * acc_ref[...]


def _peer_dense(x2d, mod_l, ht, u_tab, v_tab, r1, g1, n0, f0, B, S):
    T, D = x2d.shape
    E = u_tab.shape[0]
    tb = min(512, S)
    eb = 1024
    nper = S // tb
    tok = pl.BlockSpec((PEER_HEADS, N_KEYS, tb), lambda t, e: (0, 0, t))
    rows = pl.BlockSpec((PEER_HEADS, eb // N_KEYS, tb), lambda t, e: (0, e, t))
    return pl.pallas_call(
        functools.partial(_peer_dense_kernel, tb=tb, eb=eb),
        grid=(T // tb, E // eb),
        in_specs=[
            pl.BlockSpec((tb, D), lambda t, e: (t, 0)),
            pl.BlockSpec((None, 6, D), lambda t, e: (t // nper, 0, 0)),
            pl.BlockSpec((D, tb), lambda t, e: (0, t)),
            pl.BlockSpec((eb, D), lambda t, e: (e, 0)),
            pl.BlockSpec((eb, D), lambda t, e: (e, 0)),
            tok, tok, rows, rows,
        ],
        out_specs=pl.BlockSpec((tb, D), lambda t, e: (t, 0)),
        out_shape=jax.ShapeDtypeStruct((T, D), F32),
        scratch_shapes=[pltpu.VMEM((tb, D), F32), pltpu.VMEM((eb, tb), WDT), pltpu.VMEM((2, MXU_DIM, tb), F32)],
        compiler_params=_cparams(("parallel", "arbitrary")),
        name="peer_dense",
    )(x2d, mod_l, ht, u_tab.astype(BF16), v_tab.astype(BF16), r1, g1, n0, f0)


def _rwkv_pre_kernel(x_ref, xp_ref, xn_ref, mod_ref, g_ref, mix_ref, wr_ref, wk_ref, wv_ref, g1_ref, g2_ref,
                     w1_ref, w2_ref, a1_ref, a2_ref, w0_ref, a0_ref, kk_ref, ka_ref, rk_ref, bd_ref,
                     r_o, v_o, g_o, kn_o, lw_o, kz_o, az_o, bon_o, *, nper):
    i = pl.program_id(0)
    gn = g_ref[...]
    sh = mod_ref[0:1, :]
    sc = mod_ref[1:2, :]
    h = _modnorm(x_ref[...], gn, sh, sc)
    tm = h.shape[0]
    hp = _modnorm(xp_ref[...], gn, sh, sc)[7:8, :]
    hn = _modnorm(xn_ref[...], gn, sh, sc)[0:1, :]
    hp = jnp.where(i % nper == 0, 0.0, hp)
    hn = jnp.where(i % nper == nper - 1, 0.0, hn)
    rows = lax.broadcasted_iota(jnp.int32, h.shape, 0)
    prev = jnp.where(rows == 0, hp, pltpu.roll(h, 1, 0))
    nxt = jnp.where(rows == tm - 1, hn, pltpu.roll(h, tm - 1, 0))
    xx = 0.5 * (prev + nxt) - h
    mix = mix_ref[...]
    xr, xw, xk, xv, xa, xg = ((h + xx * mix[j:j + 1, :]).astype(BF16) for j in range(6))
    r = _dot(xr, wr_ref[...])
    k = _dot(xk, wk_ref[...])
    v = _dot(xv, wv_ref[...])
    gl = _dot(xg, g1_ref[...])
    g = _dot((1.0 / (1.0 + jnp.exp(-gl))).astype(BF16), g2_ref[...])
    wl = _dot(jnp.tanh(_dot(xw, w1_ref[...])).astype(BF16), w2_ref[...])
    al = _dot(_dot(xa, a1_ref[...]).astype(BF16), a2_ref[...])
    bd = bd_ref[...]
    kraw = k * kk_ref[...]
    kn = kraw / jnp.maximum(jnp.sqrt(_segsum(kraw * kraw, bd)), 1e-12)
    r_o[...] = r
    v_o[...] = v
    g_o[...] = g
    kn_o[...] = kn
    D = r.shape[1]
    bon = jnp.zeros_like(r)
    for z in range(2):
        u = -(w0_ref[z:z + 1, :] + wl[:, z * D:(z + 1) * D])
        softplus = jnp.maximum(u, 0.0) + jnp.log(1.0 + jnp.exp(-jnp.abs(u)))
        lw_o[z] = -jnp.exp(-softplus - 0.5)
        a = 1.0 / (1.0 + jnp.exp(-(a0_ref[z:z + 1, :] + al[:, z * D:(z + 1) * D])))
        kz = k * (1.0 + (a - 1.0) * ka_ref[...])
        kz_o[z] = kz
        az_o[z] = a
        bon = bon + _segsum(r * kz * rk_ref[...], bd) * v
    bon_o[...] = bon


def _blockdiag2(w):
    n, D = w.shape[1], w.shape[2]
    z = jnp.zeros((n, D), w.dtype)
    return jnp.concatenate([jnp.concatenate([w[0], z], axis=1), jnp.concatenate([z, w[1]], axis=1)], axis=0)


def _rwkv_pre(x2d, mod_l, norm_g, mix, w_rkv, w0, w1, w2, a0, a1, a2, g1, g2, k_k, k_a, r_k, B, S):
    T, D = x2d.shape
    tm = 256
    nper = S // tm
    nblk8 = T // 8
    bd = jnp.kron(jnp.eye(MXU_DIM // RWKV_HEAD, dtype=F32), jnp.ones((RWKV_HEAD, RWKV_HEAD), F32)).astype(BF16)
    w1c = jnp.concatenate([w1[0], w1[1]], axis=1).astype(BF16)
    a1c = jnp.concatenate([a1[0], a1[1]], axis=1).astype(BF16)
    w2b = _blockdiag2(w2).astype(BF16)
    a2b = _blockdiag2(a2).astype(BF16)
    const = lambda shp: pl.BlockSpec(shp, lambda i: (0,) * len(shp), pipeline_mode=pl.Buffered(1))
    row = pl.BlockSpec((tm, D), lambda i: (i, 0))
    row2 = pl.BlockSpec((2, tm, D), lambda i: (0, i, 0))
    return pl.pallas_call(
        functools.partial(_rwkv_pre_kernel, nper=nper),
        grid=(T // tm,),
        in_specs=[
            row,
            pl.BlockSpec((8, D), lambda i: (jnp.maximum(i * (tm // 8) - 1, 0), 0)),
            pl.BlockSpec((8, D), lambda i: (jnp.minimum((i + 1) * (tm // 8), nblk8 - 1), 0)),
            pl.BlockSpec((None, 6, D), lambda i: (i // nper, 0, 0)),
            const((1, D)), const((6, D)),
            const((D, D)), const((D, D)), const((D, D)),
            const(g1.shape), const(g2.shape),
            const(w1c.shape), const(w2b.shape), const(a1c.shape), const(a2b.shape),
            const((2, D)), const((2, D)), const((1, D)), const((1, D)), const((1, D)),
            const((MXU_DIM, MXU_DIM)),
        ],
        out_specs=[row, row, row, row, row2, row2, row2, row],
        out_shape=[jax.ShapeDtypeStruct((T, D), F32)] * 4 + [jax.ShapeDtypeStruct((2, T, D), F32)] * 3
                  + [jax.ShapeDtypeStruct((T, D), F32)],
        compiler_params=_cparams(("parallel",)),
        name="l1_rwkv_proj",
    )(x2d, x2d, x2d, mod_l, norm_g.reshape(1, D), mix,
      w_rkv[0].astype(BF16), w_rkv[1].astype(BF16), w_rkv[2].astype(BF16), g1.astype(BF16), g2.astype(BF16),
      w1c, w2b, a1c, a2b, w0, a0, k_k.reshape(1, D), k_a.reshape(1, D), r_k.reshape(1, D), bd)


SCAN_C = 64
SCAN_W = 256
SCAN_HEADS = SCAN_W // RWKV_HEAD


def _scan_chunk(refs, cs, s_ref, reverse):
    r, v, kn, lw, kz, az = (ref[:, cs] for ref in refs)
    S = s_ref[...]
    C = SCAN_C
    n = SCAN_HEADS * C
    ti = lax.broadcasted_iota(jnp.int32, (C, C), 0)
    si = lax.broadcasted_iota(jnp.int32, (C, C), 1)
    inc = jnp.where((si >= ti) if reverse else (si <= ti), 1.0, 0.0).astype(BF16)
    lw_hi = lw.astype(BF16)
    lw_lo = (lw - lw_hi.astype(F32)).astype(BF16)
    cum = _dot(inc, lw_hi) + _dot(inc, lw_lo)
    yield
    tot = cum[0:1, :] if reverse else cum[C - 1:C, :]
    a = -kn
    b = kn * az
    pinv = jnp.exp(-cum)
    pend = jnp.exp(tot - cum)
    rt = r * jnp.exp(cum)
    at = a * jnp.exp(cum - lw)
    bt = (b * pinv).astype(BF16)
    kt = (kz * pinv).astype(BF16)

    lane = lax.broadcasted_iota(jnp.int32, (C, SCAN_W), 1) // RWKV_HEAD
    stack = lambda t: jnp.concatenate([jnp.where(lane == hh, t, 0.0) for hh in range(SCAN_HEADS)], axis=0)
    unstack = lambda t: sum(jnp.where(lane == hh, t[hh * C:(hh + 1) * C], 0.0) for hh in range(SCAN_HEADS))
    half = lax.broadcasted_iota(jnp.int32, (C, LANES), 1) // RWKV_HEAD
    swap = lambda t: pltpu.roll(t, RWKV_HEAD, 1)

    def flat_to_local(t):
        tiles = [t[:, 0:LANES], t[:, LANES:2 * LANES]]
        return jnp.concatenate([tiles[hh // 2] if hh % 2 == 0 else swap(tiles[hh // 2])
                                for hh in range(SCAN_HEADS)], axis=0)

    def local_to_flat(t):
        blk = lambda hh: t[hh * C:(hh + 1) * C]
        return jnp.concatenate([jnp.where(half == 0, blk(2 * j), swap(blk(2 * j + 1)))
                                for j in range(SCAN_HEADS // 2)], axis=1)

    def local_to_blockdiag(t, slot):
        rows = []
        zero = jnp.zeros((C, LANES), F32)
        for hh in range(SCAN_HEADS):
            p = t[hh * C:(hh + 1) * C]
            if hh % 2 != slot:
                p = swap(p)
            p = jnp.where(half == hh % 2, p, 0.0)
            rows.append(jnp.concatenate([p, zero] if hh < 2 else [zero, p], axis=1))
        return jnp.concatenate(rows, axis=0).astype(BF16)

    lhs = jnp.concatenate([stack(at), stack(rt)], axis=0).astype(BF16)
    prod = _dot_nt(lhs, jnp.concatenate([kt, bt], axis=0))
    yield
    tt = lax.broadcasted_iota(jnp.int32, (n, LANES), 0) % C
    ss = lax.broadcasted_iota(jnp.int32, (n, LANES), 1) % C
    a_loc = jnp.where((ss > tt) if reverse else (ss < tt), prod[:n], 0.0)
    r_loc = jnp.where((ss >= tt) if reverse else (ss <= tt), prod[n:], 0.0)
    x0 = _dot_nt(jnp.concatenate([at, rt], axis=0).astype(BF16), S.astype(BF16))
    yield
    a_s0 = x0[:C]
    r_s0 = x0[C:]
    slot0 = lax.broadcasted_iota(jnp.int32, (n, LANES), 1) < RWKV_HEAD
    akv = _dot(local_to_blockdiag(a_loc, 0), flat_to_local(v).astype(BF16))
    yield
    xa = jnp.where(slot0, flat_to_local(a_s0) + akv, a_loc)
    ap = local_to_blockdiag(a_loc, 1)
    nsteps = int(math.log2(C))
    for it in range(nsteps):
        step = _dot(ap, xa.astype(BF16))
        yield
        xa = jnp.where(slot0, xa + step, step)
        if it + 1 < nsteps:
            ap = local_to_blockdiag(step, 1)
    u = local_to_flat(xa)
    y_s = _dot(r_loc.astype(BF16), jnp.concatenate([v, u], axis=0).astype(BF16))
    yield
    y = r_s0 + unstack(y_s)
    uv = jnp.concatenate([u, v], axis=0)
    bk = jnp.concatenate([b * pend, kz * pend], axis=0)
    upd = _dot(uv.T.astype(BF16), bk.astype(BF16))
    yield
    vi = lax.broadcasted_iota(jnp.int32, (SCAN_W, SCAN_W), 0) // RWKV_HEAD
    ki = lax.broadcasted_iota(jnp.int32, (SCAN_W, SCAN_W), 1) // RWKV_HEAD
    s_ref[...] = S * jnp.exp(tot) + jnp.where(vi == ki, upd, 0.0)
    return y


SCAN_GROUPS = 4


def _scan_kernel(rf, vf, knf, lwf, kzf, azf, rb, vb, knb, lwb, kzb, azb, yf_o, yb_o, s_ref):
    @pl.when(pl.program_id(2) == 0)
    def _():
        s_ref[...] = jnp.zeros_like(s_ref)

    chains = []
    for g in range(SCAN_GROUPS):
        cs = slice(g * SCAN_W, (g + 1) * SCAN_W)
        chains.append((_scan_chunk((rf, vf, knf, lwf, kzf, azf), cs, s_ref.at[0, g], False), yf_o, cs))
        chains.append((_scan_chunk((rb, vb, knb, lwb, kzb, azb), cs, s_ref.at[1, g], True), yb_o, cs))
    while chains:
        for chain in list(chains):
            gen, out_ref, cs = chain
            try:
                next(gen)
            except StopIteration as done:
                out_ref[:, cs] = done.value
                chains.remove(chain)


def _rwkv_scan(r, v, kn, lw, kz, az, B, S):
    T, D = r.shape
    C = SCAN_C
    nc = S // C
    W = SCAN_W * SCAN_GROUPS
    ng = D // W
    fwd = lambda b, g, c: (b * nc + c, g)
    bwd = lambda b, g, c: (b * nc + nc - 1 - c, g)
    fwd3 = lambda z: (lambda b, g, c: (z, b * nc + c, g))
    bwd3 = lambda z: (lambda b, g, c: (z, b * nc + nc - 1 - c, g))
    s2 = lambda im: pl.BlockSpec((C, W), im)
    s3 = lambda im: pl.BlockSpec((None, C, W), im)
    return pl.pallas_call(
        _scan_kernel,
        grid=(B, ng, nc),
        in_specs=[s2(fwd), s2(fwd), s2(fwd), s3(fwd3(0)), s3(fwd3(0)), s3(fwd3(0)),
                  s2(bwd), s2(bwd), s2(bwd), s3(bwd3(1)), s3(bwd3(1)), s3(bwd3(1))],
        out_specs=[s2(fwd), s2(bwd)],
        out_shape=[jax.ShapeDtypeStruct((T, D), F32)] * 2,
        scratch_shapes=[pltpu.VMEM((2, SCAN_GROUPS, SCAN_W, SCAN_W), F32)],
        compiler_params=_cparams(("parallel", "parallel", "arbitrary")),
        name="l1_wkv_scan",
    )(r, v, kn, lw, kz, az, r, v, kn, lw, kz, az)


def _rwkv_out_kernel(yf_ref, yb_ref, bon_ref, g_ref, lg_ref, lb_ref, bd_ref, o_ref):
    y = yf_ref[...] + yb_ref[...]
    bd = bd_ref[...]
    mu = _segsum(y, bd) * (1.0 / RWKV_HEAD)
    yc = y - mu
    var = _segsum(yc * yc, bd) * (1.0 / RWKV_HEAD)
    yn = yc * lax.rsqrt(var + GN_EPS) * lg_ref[...] + lb_ref[...]
    o_ref[...] = ((yn + bon_ref[...]) * g_ref[...]).astype(BF16)


def _rwkv_out(yf, yb, bon, g, ln_g, ln_b):
    T, D = yf.shape
    tm = 256
    bd = jnp.kron(jnp.eye(MXU_DIM // RWKV_HEAD, dtype=F32), jnp.ones((RWKV_HEAD, RWKV_HEAD), F32)).astype(BF16)
    row = pl.BlockSpec((tm, D), lambda i: (i, 0))
    const = lambda shp: pl.BlockSpec(shp, lambda i: (0,) * len(shp))
    return pl.pallas_call(
        _rwkv_out_kernel,
        grid=(T // tm,),
        in_specs=[row, row, row, row, const((1, D)), const((1, D)), const((MXU_DIM, MXU_DIM))],
        out_specs=row,
        out_shape=jax.ShapeDtypeStruct((T, D), BF16),
        compiler_params=_cparams(("parallel",)),
        name="l1_rwkv_norm_gate",
    )(yf, yb, bon, g, ln_g.reshape(1, D), ln_b.reshape(1, D), bd)


def _peer_layer(x2d, y2d, w_o, mod_l, norm_g2, w_q, subkeys, u_tab, v_tab, B, S):
    x1, ht, st = _post(x2d, y2d, w_o, mod_l, norm_g2, w_q, subkeys, B, S)
    r1, g1, n0, f0 = _topk(st)
    return _peer_dense(x1, mod_l, ht, u_tab, v_tab, r1, g1, n0, f0, B, S)


def kernel(x, c, ada_w, ada_b, norm_g, attn_w_in, attn_qk_g, diff_lambda, diff_subln_g, attn_w_out, rel_bias,
           rwkv_mix, rwkv_w_rkv, rwkv_w0, rwkv_w1, rwkv_w2, rwkv_a0, rwkv_a1, rwkv_a2, rwkv_g1, rwkv_g2,
           rwkv_k_k, rwkv_k_a, rwkv_r_k, rwkv_ln_g, rwkv_ln_b, rwkv_w_o,
           peer_w_q, peer_subkeys, peer_u, peer_v):
    B, S, D = x.shape
    mod = _ada(c, ada_w, ada_b)
    x2d = x.reshape(B * S, D)
    for layer in range(DEPTH):
        j = layer // 2
        mod_l = mod[layer]
        if layer % 2 == 0:
            lambda_init = 0.8 - 0.6 * math.exp(-0.3 * layer)
            qa, ka, va, qb, kb, vb = _inproj(x2d, mod_l, norm_g[layer, 0], attn_w_in[j], attn_qk_g[j], B, S)
            oa = _attn_a(qa, ka, va, diff_lambda[j], diff_subln_g[j], rel_bias, lambda_init, B, S)
            ob = _attn_b(qb, kb, vb, B, S)
            y2d = jnp.concatenate([oa, ob], axis=1)
            w_o = attn_w_out[j]
        else:
            r, v, g, kn, lw, kz, az, bon = _rwkv_pre(
                x2d, mod_l, norm_g[layer, 0], rwkv_mix[j], rwkv_w_rkv[j], rwkv_w0[j], rwkv_w1[j], rwkv_w2[j],
                rwkv_a0[j], rwkv_a1[j], rwkv_a2[j], rwkv_g1[j], rwkv_g2[j], rwkv_k_k[j], rwkv_k_a[j], rwkv_r_k[j], B, S)
            yf, yb = _rwkv_scan(r, v, kn, lw, kz, az, B, S)
            y2d = _rwkv_out(yf, yb, bon, g, rwkv_ln_g[j], rwkv_ln_b[j])
            w_o = rwkv_w_o[j]
        x2d = _peer_layer(x2d, y2d, w_o, mod_l, norm_g[layer, 1], peer_w_q[layer], peer_subkeys[layer],
                          peer_u[layer], peer_v[layer], B, S)
    return x2d.reshape(B, S, D)
```

```python
import functools
import math

import jax
import jax.numpy as jnp
from jax import lax
from jax.experimental import pallas as pl
from jax.experimental.pallas import tpu as pltpu

F32 = jnp.float32
BF16 = jnp.bfloat16
HIGHEST = lax.Precision.HIGHEST

D_MODEL = 1024
DEPTH = 2
HEAD_DIM = 64
A_HEADS = 4
A_MAPS = 8
B_Q_HEADS = 8
B_KV_HEADS = 2
GRID_W = 64
ROPE_AXIS_DIM = 32
ROPE_THETA = 10000.0
REL_BUCKETS = 32
REL_MAX_DIST = 128
RWKV_HEAD = 64
GN_EPS = 64e-5
PEER_HEADS = 8
N_KEYS = 128
PEER_TOPK = 16
NORM_EPS = 1e-6

LANES = 128
MXU_DIM = 256
VMEM_LIMIT = 56 * 1024 * 1024

NT_DIMS = (((1,), (1,)), ((), ()))
LOG2E = math.log2(math.e)


def _cparams(sem):
    return pltpu.CompilerParams(dimension_semantics=sem, vmem_limit_bytes=VMEM_LIMIT)


def _dot(a, b):
    return jnp.dot(a, b, preferred_element_type=F32)


def _dot_nt(a, b):
    return lax.dot_general(a, b, NT_DIMS, preferred_element_type=F32)


def _segsum(v, bd):
    outs = []
    n = v.shape[1]
    for c0 in range(0, n, MXU_DIM):
        w = min(MXU_DIM, n - c0)
        blk = v[:, c0:c0 + w]
        hi = blk.astype(BF16)
        lo = (blk - hi.astype(F32)).astype(BF16)
        b = bd[:w, :w]
        outs.append(_dot(hi, b) + _dot(lo, b))
    return outs[0] if len(outs) == 1 else jnp.concatenate(outs, axis=1)


def _modnorm(x, g, sh, sc):
    ms = jnp.mean(x * x, axis=-1, keepdims=True)
    return (x * lax.rsqrt(ms + NORM_EPS) * g) * (1.0 + sc) + sh


def _ada_kernel(c_ref, w_ref, b_ref, o_ref):
    c = c_ref[...]
    cs = c * (1.0 / (1.0 + jnp.exp(-c)))
    o_ref[...] = jnp.dot(cs, w_ref[...], preferred_element_type=F32, precision=HIGHEST) + b_ref[...]


def _ada(c, ada_w, ada_b):
    B, D = c.shape
    n6 = ada_w.shape[-1]
    tn = 1536
    out = pl.pallas_call(
        _ada_kernel,
        grid=(DEPTH, n6 // tn),
        in_specs=[
            pl.BlockSpec((B, D), lambda l, j: (0, 0)),
            pl.BlockSpec((None, D, tn), lambda l, j: (l, 0, j)),
            pl.BlockSpec((None, 1, tn), lambda l, j: (l, 0, j)),
        ],
        out_specs=pl.BlockSpec((None, B, tn), lambda l, j: (l, 0, j)),
        out_shape=jax.ShapeDtypeStruct((DEPTH, B, n6), F32),
        compiler_params=_cparams(("parallel", "parallel")),
        name="ada_mod",
    )(c, ada_w, ada_b.reshape(DEPTH, 1, n6))
    return out.reshape(DEPTH, B, 6, D)


C_QA, C_KA, C_VA, C_QB, C_KB, C_VB, C_QBS, C_KBS, C_END = 0, 512, 1024, 1536, 2048, 2304, 2560, 3072, 3328


def _inproj_kernel(x_ref, mod_ref, g_ref, w_ref, gain_ref, cos_ref, sin_ref, bd_ref,
                   qa_ref, ka_ref, va_ref, qb_ref, kb_ref, vb_ref):
    h = _modnorm(x_ref[...], g_ref[...], mod_ref[0:1, :], mod_ref[1:2, :])
    y = _dot(h.astype(BF16), w_ref[...])
    bd = bd_ref[...]
    gain = gain_ref[...]

    def rinv(sec):
        return lax.rsqrt(_segsum(sec * sec, bd) * (1.0 / HEAD_DIM) + NORM_EPS)

    yqa = y[:, C_QA:C_KA]
    qa_ref[...] = (yqa * rinv(yqa) * gain[:, C_QA:C_KA]).astype(BF16)
    yka = y[:, C_KA:C_VA]
    ka_ref[...] = (yka * rinv(yka) * gain[:, C_KA:C_VA]).astype(BF16)
    va_ref[...] = y[:, C_VA:C_QB].astype(BF16)
    cos = cos_ref[...]
    sin = sin_ref[...]
    yqb = y[:, C_QB:C_KB]
    qb = (yqb * gain[:, C_QB:C_KB] * cos + y[:, C_QBS:C_KBS] * gain[:, C_QBS:C_KBS] * sin) * rinv(yqb)
    qb_ref[...] = qb.astype(BF16)
    ykb = y[:, C_KB:C_VB]
    kb = (ykb * gain[:, C_KB:C_VB] * cos[:, :256] + y[:, C_KBS:C_END] * gain[:, C_KBS:C_END] * sin[:, :256]) * rinv(ykb)
    kb_ref[...] = kb.astype(BF16)
    vb_ref[...] = y[:, C_VB:C_QBS].astype(BF16)


def _swap16(a):
    sh = a.shape
    a = a.reshape(sh[:-1] + (sh[-1] // 32, 2, 16))
    return a[..., ::-1, :].reshape(sh)


def _inproj(x2d, mod_l, norm_g, w_in, qk_g, B, S):
    T, D = x2d.shape
    tm = 256
    nper = S // tm
    scale = HEAD_DIM ** -0.5 * LOG2E
    wqa, wka, wva, wqb, wkb, wvb = (w_in[:, a:b] for a, b in
                                    ((0, 512), (512, 1024), (1024, 1536), (1536, 2048), (2048, 2176), (2176, 2304)))
    dup = lambda w: jnp.repeat(w.reshape(D, B_KV_HEADS, 1, HEAD_DIM), 2, axis=2).reshape(D, 4 * HEAD_DIM)
    wkbd = dup(wkb)
    w_all = jnp.concatenate([wqa, wka, wva, wqb, wkbd, dup(wvb), _swap16(wqb), _swap16(wkbd)], axis=1).astype(BF16)
    gq = jnp.tile(qk_g[2], B_Q_HEADS) * scale
    gk = jnp.tile(qk_g[3], 4)
    gain = jnp.concatenate([jnp.tile(qk_g[0], A_MAPS) * scale, jnp.tile(qk_g[1], A_MAPS), jnp.ones((512,), F32),
                            gq, gk, jnp.ones((256,), F32), _swap16(gq), _swap16(gk)]).reshape(1, C_END)
    pos = jnp.arange(S)
    row = (pos // GRID_W).astype(F32)
    col = (pos % GRID_W).astype(F32)
    inv = ROPE_THETA ** (-jnp.arange(0, ROPE_AXIS_DIM, 2, dtype=F32) / ROPE_AXIS_DIM)
    ang = jnp.concatenate([row[:, None] * inv] * 2 + [col[:, None] * inv] * 2, axis=1)
    sign = jnp.tile(jnp.concatenate([-jnp.ones((16,), F32), jnp.ones((16,), F32)]), 2)
    cos_t = jnp.tile(jnp.cos(ang), (1, B_Q_HEADS))
    sin_t = jnp.tile(jnp.sin(ang) * sign, (1, B_Q_HEADS))
    bd = jnp.kron(jnp.eye(MXU_DIM // HEAD_DIM, dtype=F32), jnp.ones((HEAD_DIM, HEAD_DIM), F32)).astype(BF16)
    row_spec = lambda w: pl.BlockSpec((tm, w), lambda i: (i, 0))
    const = lambda shp: pl.BlockSpec(shp, lambda i: (0,) * len(shp))
    outs = pl.pallas_call(
        _inproj_kernel,
        grid=(T // tm,),
        in_specs=[
            row_spec(D),
            pl.BlockSpec((None, 6, D), lambda i: (i // nper, 0, 0)),
            const((1, D)),
            const((D, C_END)),
            const((1, C_END)),
            pl.BlockSpec((tm, 512), lambda i: (i % nper, 0)),
            pl.BlockSpec((tm, 512), lambda i: (i % nper, 0)),
            const((MXU_DIM, MXU_DIM)),
        ],
        out_specs=[row_spec(512), row_spec(512), row_spec(512), row_spec(512), row_spec(256), row_spec(256)],
        out_shape=[jax.ShapeDtypeStruct((T, w), BF16) for w in (512, 512, 512, 512, 256, 256)],
        compiler_params=_cparams(("parallel",)),
        name="l0_inproj",
    )(x2d, mod_l, norm_g.reshape(1, D), w_all, gain, cos_t, sin_t, bd)
    return outs


def _t5_bucket(rel):
    nb = REL_BUCKETS // 2
    max_exact = nb // 2
    ret = jnp.where(rel > 0, nb, 0)
    n = jnp.abs(rel)
    nf = jnp.maximum(n, 1).astype(F32)
    large = max_exact + (jnp.log(nf / max_exact) / math.log(REL_MAX_DIST / max_exact) * (nb - max_exact)).astype(jnp.int32)
    large = jnp.minimum(large, nb - 1)
    return ret + jnp.where(n < max_exact, n, large)


def _softmax_parts(s):
    m = jnp.max(s, axis=1, keepdims=True)
    e = jnp.exp2(s - m)
    return e, 1.0 / jnp.sum(e, axis=1, keepdims=True)


def _attn_a_kernel(lam_ref, q_ref, k_ref, v_ref, bias_ref, sg_ref, o_ref, *, tq, nsub, S, lambda_init):
    i = pl.program_id(2)
    lv = lam_ref[...]
    lam = (jnp.exp(jnp.sum(lv[0:1] * lv[1:2], axis=1, keepdims=True))
           - jnp.exp(jnp.sum(lv[2:3] * lv[3:4], axis=1, keepdims=True)) + lambda_init)
    k = k_ref[...]
    v = v_ref[...]
    lane = lax.broadcasted_iota(jnp.int32, (tq, LANES), 1)
    nsl = tq // LANES + 4

    def scores(sub):
        q = q_ref[sub * tq:(sub + 1) * tq, :]
        return [_dot_nt(jnp.where((lane < HEAD_DIM) == (mi == 0), q, jnp.zeros_like(q)), k) for mi in range(2)]

    s_next = scores(0)
    for sub in range(nsub):
        ss = s_next
        if sub + 1 < nsub:
            s_next = scores(sub + 1)
        blk = i * nsub + sub
        ps = []
        for mi in range(2):
            tiles = []
            for j in range(S // LANES):
                idx = jnp.clip(j - blk * (tq // LANES) + 2, 0, nsl - 1)
                tiles.append(bias_ref[mi, idx])
            ps.append(_softmax_parts(ss[mi] + jnp.concatenate(tiles, axis=1)))
        diff = (ps[0][0] * ps[0][1] - ps[1][0] * (lam * ps[1][1])).astype(BF16)
        oa = _dot(diff, v)
        ms = jnp.mean(oa * oa, axis=1, keepdims=True)
        o_ref[sub * tq:(sub + 1) * tq, :] = (oa * lax.rsqrt(ms + NORM_EPS) * sg_ref[...]
                                             * (1.0 - lambda_init)).astype(BF16)


def _attn_a(qa, ka, va, diff_lambda, subln_g, rel_bias, lambda_init, B, S):
    tq = 256
    nsub = 2 if S % (2 * tq) == 0 else 1
    tqb = tq * nsub
    nq = S // tqb
    nsl = tq // LANES + 4
    d = jnp.arange(nsl) - 2
    rel = d[:, None, None] * LANES + jnp.arange(LANES)[None, None, :] - jnp.arange(tq)[None, :, None]
    rel = jnp.where(d[:, None, None] < -1, -S, jnp.where(d[:, None, None] > tq // LANES, S, rel))
    onehot = (_t5_bucket(rel)[None] == jnp.arange(REL_BUCKETS)[:, None, None, None]).astype(F32)
    bias_tab = jnp.einsum("bm,bsqc->msqc", rel_bias.astype(F32) * LOG2E, onehot, precision=HIGHEST)
    kern = functools.partial(_attn_a_kernel, tq=tq, nsub=nsub, S=S, lambda_init=lambda_init)
    return pl.pallas_call(
        kern,
        grid=(B, A_HEADS, nq),
        in_specs=[
            pl.BlockSpec((4, HEAD_DIM), lambda b, h, i: (0, 0)),
            pl.BlockSpec((tqb, LANES), lambda b, h, i: (b * nq + i, h)),
            pl.BlockSpec((S, LANES), lambda b, h, i: (b, h)),
            pl.BlockSpec((S, LANES), lambda b, h, i: (b, h)),
            pl.BlockSpec((2, nsl, tq, LANES), lambda b, h, i: (h, 0, 0, 0)),
            pl.BlockSpec((1, LANES), lambda b, h, i: (0, 0)),
        ],
        out_specs=pl.BlockSpec((tqb, LANES), lambda b, h, i: (b * nq + i, h)),
        out_shape=jax.ShapeDtypeStruct((B * S, A_HEADS * LANES), BF16),
        compiler_params=_cparams(("parallel", "parallel", "arbitrary")),
        name="l0_attn_diff",
    )(diff_lambda, qa, ka, va, bias_tab, subln_g.reshape(1, LANES))


def _attn_b_kernel(q_ref, k_ref, v_ref, o_ref):
    q = q_ref[...]
    k = k_ref[...]
    v = v_ref[...]
    lane = lax.broadcasted_iota(jnp.int32, (q.shape[0], LANES), 1)

    def scores(r):
        qp = q[:, (r // 2) * LANES:(r // 2 + 1) * LANES]
        qm = jnp.where((lane < HEAD_DIM) == (r % 2 == 0), qp, jnp.zeros_like(qp))
        return _dot_nt(qm, k)

    nh = B_Q_HEADS // B_KV_HEADS
    heads = []
    s_next = scores(0)
    for r in range(nh):
        s = s_next
        if r + 1 < nh:
            s_next = scores(r + 1)
        e, rinv = _softmax_parts(s)
        heads.append(_dot(e.astype(BF16), v) * rinv)
    outs = [jnp.where(lane < HEAD_DIM, heads[2 * u], heads[2 * u + 1]) for u in range(nh // 2)]
    o_ref[...] = jnp.concatenate(outs, axis=1).astype(BF16)


def _attn_b(qb, kb, vb, B, S):
    tq = 256
    nq = S // tq
    return pl.pallas_call(
        _attn_b_kernel,
        grid=(B, B_KV_HEADS, nq),
        in_specs=[
            pl.BlockSpec((tq, 2 * LANES), lambda b, g, i: (b * nq + i, g)),
            pl.BlockSpec((S, LANES), lambda b, g, i: (b, g)),
            pl.BlockSpec((S, LANES), lambda b, g, i: (b, g)),
        ],
        out_specs=pl.BlockSpec((tq, 2 * LANES), lambda b, g, i: (b * nq + i, g)),
        out_shape=jax.ShapeDtypeStruct((B * S, B_Q_HEADS * HEAD_DIM), BF16),
        compiler_params=_cparams(("parallel", "parallel", "arbitrary")),
        name="l0_attn_gqa",
    )(qb, kb, vb)


def _post_kernel(x_ref, ya_ref, yb_ref, wa_ref, wb_ref, mod_ref, g_ref, wq_ref, sk_ref, x1_ref, ht_ref, st_ref):
    y = _dot(ya_ref[...], wa_ref[...]) + _dot(yb_ref[...], wb_ref[...])
    x1 = x_ref[...] + mod_ref[2:3, :] * y
    x1_ref[...] = x1
    h2 = _modnorm(x1, g_ref[...], mod_ref[3:4, :], mod_ref[4:5, :])
    ht_ref[...] = h2.T.astype(BF16)
    q = _dot(h2.astype(BF16), wq_ref[...])
    for hp in range(2 * PEER_HEADS):
        qs = q[:, hp * N_KEYS:(hp + 1) * N_KEYS].astype(BF16)
        st_ref[hp] = _dot_nt(sk_ref[hp], qs)


def _post(x2d, y2d, w_o, mod_l, norm_g, w_q, subkeys, B, S):
    T, D = x2d.shape
    tm = 256
    nper = S // tm
    half = y2d.shape[1] // 2
    nq = w_q.shape[1]
    w_o = w_o.astype(BF16)
    sk = subkeys.reshape(2 * PEER_HEADS, N_KEYS, N_KEYS).astype(BF16)
    const = lambda shp: pl.BlockSpec(shp, lambda i: (0,) * len(shp))
    return pl.pallas_call(
        _post_kernel,
        grid=(T // tm,),
        in_specs=[
            pl.BlockSpec((tm, D), lambda i: (i, 0)),
            pl.BlockSpec((tm, half), lambda i: (i, 0)),
            pl.BlockSpec((tm, half), lambda i: (i, 1)),
            const((half, D)),
            const((half, D)),
            pl.BlockSpec((None, 6, D), lambda i: (i // nper, 0, 0)),
            const((1, D)),
            const((D, nq)),
            const((2 * PEER_HEADS, N_KEYS, N_KEYS)),
        ],
        out_specs=[
            pl.BlockSpec((tm, D), lambda i: (i, 0)),
            pl.BlockSpec((D, tm), lambda i: (0, i)),
            pl.BlockSpec((2 * PEER_HEADS, N_KEYS, tm), lambda i: (0, 0, i)),
        ],
        out_shape=[
            jax.ShapeDtypeStruct((T, D), F32),
            jax.ShapeDtypeStruct((D, T), BF16),
            jax.ShapeDtypeStruct((2 * PEER_HEADS, N_KEYS, T), F32),
        ],
        compiler_params=_cparams(("parallel",)),
        name="mix_out_peer_q",
    )(x2d, y2d, y2d, w_o[:half], w_o[half:], mod_l, norm_g.reshape(1, D), w_q.astype(BF16), sk)


_STAIR = [(i, j) for i in range(PEER_TOPK) for j in range(PEER_TOPK) if (i + 1) * (j + 1) <= PEER_TOPK]
_STAIR_ROWS = 56
_NEG = float("-inf")
WDT = BF16


def _extract_top(s, iota, nsel, sentinel, exact):
    vals = []
    rank = jnp.full(s.shape, float(nsel), F32)
    for r in range(nsel):
        m = jnp.max(s, axis=0, keepdims=True)
        hit = s == m
        if exact:
            hit = iota == jnp.min(jnp.where(hit, iota, sentinel), axis=0, keepdims=True)
        vals.append(m)
        rank = jnp.where(hit, float(r), rank)
        s = jnp.where(hit, _NEG, s)
    return vals, rank


def _peer_head_select(s0, s1, iota, iota_c, group, exact):
    a, rank0 = _extract_top(s0, iota, PEER_TOPK, float(N_KEYS), exact)
    b, rank1 = _extract_top(s1, iota, PEER_TOPK, float(N_KEYS), exact)
    cand = jnp.full(iota_c.shape, _NEG, F32)
    for p, (i, j) in enumerate(_STAIR):
        cand = jnp.where(iota_c == float(p), a[i] + b[j], cand)
    _, crank = _extract_top(cand, iota_c, PEER_TOPK, float(_STAIR_ROWS), exact)
    sel = jnp.where(crank < float(PEER_TOPK), 1.0, 0.0)
    z = jnp.sum(sel * jnp.exp(cand - (a[0] + b[0])), axis=0, keepdims=True)
    n0 = jnp.zeros(s0.shape, F32)
    for i in range(PEER_TOPK):
        cnt = jnp.sum(jnp.where(group == float(i), sel, 0.0), axis=0, keepdims=True)
        n0 = jnp.where(rank0 == float(i), cnt, n0)
    marked = (jnp.sum(jnp.where(rank0 < float(PEER_TOPK), 1.0, 0.0), axis=0, keepdims=True)
              + jnp.sum(jnp.where(rank1 < float(PEER_TOPK), 1.0, 0.0), axis=0, keepdims=True)
              + jnp.sum(sel, axis=0, keepdims=True))
    clean = jnp.max(jnp.abs(marked - 3.0 * PEER_TOPK)) == 0.0
    return rank1, jnp.exp(s1 - b[0]), n0, jnp.exp(s0 - a[0]) * (1.0 / z), clean


def _topk_kernel(st_ref, r1_ref, g1_ref, n0_ref, f0_ref, *, tt):
    iota = lax.broadcasted_iota(jnp.int32, (N_KEYS, tt), 0).astype(F32)
    iota_c = lax.broadcasted_iota(jnp.int32, (_STAIR_ROWS, tt), 0).astype(F32)
    starts = [p for p, (i, j) in enumerate(_STAIR) if j == 0]
    group = jnp.zeros((_STAIR_ROWS, tt), F32)
    for st in starts[1:]:
        group = group + jnp.where(iota_c >= float(st), 1.0, 0.0)

    def store(h, rank1, g1, n0, f0):
        r1_ref[h] = rank1.astype(WDT)
        g1_ref[h] = g1.astype(WDT)
        n0_ref[h] = n0
        f0_ref[h] = f0

    def head_body(h, carry):
        s0 = st_ref[2 * h]
        s1 = st_ref[2 * h + 1]
        rank1, g1, n0, f0, clean = _peer_head_select(s0, s1, iota, iota_c, group, exact=False)
        store(h, rank1, g1, n0, f0)

        @pl.when(jnp.logical_not(clean))
        def _():
            store(h, *_peer_head_select(s0, s1, iota, iota_c, group, exact=True)[:4])

        return carry

    lax.fori_loop(0, PEER_HEADS, head_body, 0)


def _topk(st):
    T = st.shape[-1]
    tt = 512
    blk = pl.BlockSpec((PEER_HEADS, N_KEYS, tt), lambda i: (0, 0, i))
    return pl.pallas_call(
        functools.partial(_topk_kernel, tt=tt),
        grid=(T // tt,),
        in_specs=[pl.BlockSpec((2 * PEER_HEADS, N_KEYS, tt), lambda i: (0, 0, i))],
        out_specs=[blk, blk, blk, blk],
        out_shape=[jax.ShapeDtypeStruct((PEER_HEADS, N_KEYS, T), WDT)] * 2
                  + [jax.ShapeDtypeStruct((PEER_HEADS, N_KEYS, T), F32)] * 2,
        compiler_params=_cparams(("parallel",)),
        name="peer_topk",
    )(st)


_GELU_K0 = -2.0 * math.sqrt(2.0 / math.pi) * math.log2(math.e)
_GELU_K1 = _GELU_K0 * 0.044715


def _gelu_tanh(x):
    return x / (1.0 + jnp.exp2(x * (_GELU_K0 + _GELU_K1 * (x * x))))


def _peer_dense_kernel(x_ref, mod_ref, ht_ref, u_ref, v_ref, r1_ref, g1_ref, n0_ref, f0_ref, o_ref, acc_ref, z_ref, act_ref,
                       *, tb, eb):
    e = pl.program_id(1)

    @pl.when(e == 0)
    def _():
        acc_ref[...] = jnp.zeros_like(acc_ref)

    nch = eb // MXU_DIM
    chunk = lambda k: slice(k * MXU_DIM, (k + 1) * MXU_DIM)
    def activations(k):
        act_ref[k % 2] = _dot(u_ref[chunk(k), :], ht_ref[...])

    activations(0)
    for k in range(nch):
        if k + 1 < nch:
            activations(k + 1)
        for r in range(MXU_DIM // N_KEYS):
            ii = k * (MXU_DIM // N_KEYS) + r
            for c0 in range(0, tb, MXU_DIM):
                cs = slice(c0, c0 + MXU_DIM)
                w = jnp.zeros((N_KEYS, MXU_DIM), WDT)
                for h in range(PEER_HEADS):
                    n0 = n0_ref[h, ii:ii + 1, cs].astype(WDT)
                    f0 = f0_ref[h, ii:ii + 1, cs].astype(WDT)
                    w = w + jnp.where(r1_ref[h, :, cs] < n0, g1_ref[h, :, cs] * f0, jnp.zeros((), WDT))
                act = act_ref[k % 2, r * N_KEYS:(r + 1) * N_KEYS, cs]
                z_ref[ii * N_KEYS:(ii + 1) * N_KEYS, cs] = w * _gelu_tanh(act).astype(WDT)
        acc_ref[...] += lax.dot_general(z_ref[chunk(k), :], v_ref[chunk(k), :], (((0,), (0,)), ((), ())),
                                        preferred_element_type=F32)

    @pl.when(e == pl.num_programs(1) - 1)
    def _():
        o_ref[...] = x_ref[...] + mod_ref[5:6, :] * acc_ref[...]


def _peer_dense(x2d, mod_l, ht, u_tab, v_tab, r1, g1, n0, f0, B, S):
    T, D = x2d.shape
    E = u_tab.shape[0]
    tb = min(512, S)
    eb = 2048
    nper = S // tb
    tok = pl.BlockSpec((PEER_HEADS, N_KEYS, tb), lambda t, e: (0, 0, t))
    rows = pl.BlockSpec((PEER_HEADS, eb // N_KEYS, tb), lambda t, e: (0, e, t))
    return pl.pallas_call(
        functools.partial(_peer_dense_kernel, tb=tb, eb=eb),
        grid=(T // tb, E // eb),
        in_specs=[
            pl.BlockSpec((tb, D), lambda t, e: (t, 0)),
            pl.BlockSpec((None, 6, D), lambda t, e: (t // nper, 0, 0)),
            pl.BlockSpec((D, tb), lambda t, e: (0, t)),
            pl.BlockSpec((eb, D), lambda t, e: (e, 0)),
            pl.BlockSpec((eb, D), lambda t, e: (e, 0)),
            tok, tok, rows, rows,
        ],
        out_specs=pl.BlockSpec((tb, D), lambda t, e: (t, 0)),
        out_shape=jax.ShapeDtypeStruct((T, D), F32),
        scratch_shapes=[pltpu.VMEM((tb, D), F32), pltpu.VMEM((eb, tb), WDT), pltpu.VMEM((2, MXU_DIM, tb), F32)],
        compiler_params=_cparams(("parallel", "arbitrary")),
        name="peer_dense",
    )(x2d, mod_l, ht, u_tab.astype(BF16), v_tab.astype(BF16), r1, g1, n0, f0)


def _rwkv_pre_kernel(x_ref, xp_ref, xn_ref, mod_ref, g_ref, mix_ref, wr_ref, wk_ref, wv_ref, g1_ref, g2_ref,
                     w1_ref, w2_ref, a1_ref, a2_ref, w0_ref, a0_ref, kk_ref, ka_ref, rk_ref, bd_ref,
                     r_o, v_o, g_o, kn_o, lw_o, kz_o, az_o, bon_o, *, nper):
    i = pl.program_id(0)
    gn = g_ref[...]
    sh = mod_ref[0:1, :]
    sc = mod_ref[1:2, :]
    h = _modnorm(x_ref[...], gn, sh, sc)
    tm = h.shape[0]
    hp = _modnorm(xp_ref[...], gn, sh, sc)[7:8, :]
    hn = _modnorm(xn_ref[...], gn, sh, sc)[0:1, :]
    hp = jnp.where(i % nper == 0, 0.0, hp)
    hn = jnp.where(i % nper == nper - 1, 0.0, hn)
    rows = lax.broadcasted_iota(jnp.int32, h.shape, 0)
    prev = jnp.where(rows == 0, hp, pltpu.roll(h, 1, 0))
    nxt = jnp.where(rows == tm - 1, hn, pltpu.roll(h, tm - 1, 0))
    xx = 0.5 * (prev + nxt) - h
    mix = mix_ref[...]
    xr, xw, xk, xv, xa, xg = ((h + xx * mix[j:j + 1, :]).astype(BF16) for j in range(6))
    r = _dot(xr, wr_ref[...])
    k = _dot(xk, wk_ref[...])
    v = _dot(xv, wv_ref[...])
    gl = _dot(xg, g1_ref[...])
    g = _dot((1.0 / (1.0 + jnp.exp(-gl))).astype(BF16), g2_ref[...])
    wl = _dot(jnp.tanh(_dot(xw, w1_ref[...])).astype(BF16), w2_ref[...])
    al = _dot(_dot(xa, a1_ref[...]).astype(BF16), a2_ref[...])
    bd = bd_ref[...]
    kraw = k * kk_ref[...]
    kn = kraw / jnp.maximum(jnp.sqrt(_segsum(kraw * kraw, bd)), 1e-12)
    r_o[...] = r
    v_o[...] = v
    g_o[...] = g
    kn_o[...] = kn
    D = r.shape[1]
    bon = jnp.zeros_like(r)
    for z in range(2):
        u = -(w0_ref[z:z + 1, :] + wl[:, z * D:(z + 1) * D])
        softplus = jnp.maximum(u, 0.0) + jnp.log(1.0 + jnp.exp(-jnp.abs(u)))
        lw_o[z] = -jnp.exp(-softplus - 0.5)
        a = 1.0 / (1.0 + jnp.exp(-(a0_ref[z:z + 1, :] + al[:, z * D:(z + 1) * D])))
        kz = k * (1.0 + (a - 1.0) * ka_ref[...])
        kz_o[z] = kz
        az_o[z] = a
        bon = bon + _segsum(r * kz * rk_ref[...], bd) * v
    bon_o[...] = bon


def _blockdiag2(w):
    n, D = w.shape[1], w.shape[2]
    z = jnp.zeros((n, D), w.dtype)
    return jnp.concatenate([jnp.concatenate([w[0], z], axis=1), jnp.concatenate([z, w[1]], axis=1)], axis=0)


def _rwkv_pre(x2d, mod_l, norm_g, mix, w_rkv, w0, w1, w2, a0, a1, a2, g1, g2, k_k, k_a, r_k, B, S):
    T, D = x2d.shape
    tm = 256
    nper = S // tm
    nblk8 = T // 8
    bd = jnp.kron(jnp.eye(MXU_DIM // RWKV_HEAD, dtype=F32), jnp.ones((RWKV_HEAD, RWKV_HEAD), F32)).astype(BF16)
    w1c = jnp.concatenate([w1[0], w1[1]], axis=1).astype(BF16)
    a1c = jnp.concatenate([a1[0], a1[1]], axis=1).astype(BF16)
    w2b = _blockdiag2(w2).astype(BF16)
    a2b = _blockdiag2(a2).astype(BF16)
    const = lambda shp: pl.BlockSpec(shp, lambda i: (0,) * len(shp), pipeline_mode=pl.Buffered(1))
    row = pl.BlockSpec((tm, D), lambda i: (i, 0))
    row2 = pl.BlockSpec((2, tm, D), lambda i: (0, i, 0))
    return pl.pallas_call(
        functools.partial(_rwkv_pre_kernel, nper=nper),
        grid=(T // tm,),
        in_specs=[
            row,
            pl.BlockSpec((8, D), lambda i: (jnp.maximum(i * (tm // 8) - 1, 0), 0)),
            pl.BlockSpec((8, D), lambda i: (jnp.minimum((i + 1) * (tm // 8), nblk8 - 1), 0)),
            pl.BlockSpec((None, 6, D), lambda i: (i // nper, 0, 0)),
            const((1, D)), const((6, D)),
            const((D, D)), const((D, D)), const((D, D)),
            const(g1.shape), const(g2.shape),
            const(w1c.shape), const(w2b.shape), const(a1c.shape), const(a2b.shape),
            const((2, D)), const((2, D)), const((1, D)), const((1, D)), const((1, D)),
            const((MXU_DIM, MXU_DIM)),
        ],
        out_specs=[row, row, row, row, row2, row2, row2, row],
        out_shape=[jax.ShapeDtypeStruct((T, D), F32)] * 4 + [jax.ShapeDtypeStruct((2, T, D), F32)] * 3
                  + [jax.ShapeDtypeStruct((T, D), F32)],
        compiler_params=_cparams(("parallel",)),
        name="l1_rwkv_proj",
    )(x2d, x2d, x2d, mod_l, norm_g.reshape(1, D), mix,
      w_rkv[0].astype(BF16), w_rkv[1].astype(BF16), w_rkv[2].astype(BF16), g1.astype(BF16), g2.astype(BF16),
      w1c, w2b, a1c, a2b, w0, a0, k_k.reshape(1, D), k_a.reshape(1, D), r_k.reshape(1, D), bd)


SCAN_C = 64
SCAN_W = 256
SCAN_HEADS = SCAN_W // RWKV_HEAD


def _scan_chunk(refs, cs, s_ref, reverse):
    r, v, kn, lw, kz, az = (ref[:, cs] for ref in refs)
    S = s_ref[...]
    C = SCAN_C
    n = SCAN_HEADS * C
    ti = lax.broadcasted_iota(jnp.int32, (C, C), 0)
    si = lax.broadcasted_iota(jnp.int32, (C, C), 1)
    inc = jnp.where((si >= ti) if reverse else (si <= ti), 1.0, 0.0).astype(BF16)
    lw_hi = lw.astype(BF16)
    lw_lo = (lw - lw_hi.astype(F32)).astype(BF16)
    cum = _dot(inc, lw_hi) + _dot(inc, lw_lo)
    yield
    tot = cum[0:1, :] if reverse else cum[C - 1:C, :]
    a = -kn
    b = kn * az
    pinv = jnp.exp(-cum)
    pend = jnp.exp(tot - cum)
    rt = r * jnp.exp(cum)
    at = a * jnp.exp(cum - lw)
    bt = (b * pinv).astype(BF16)
    kt = (kz * pinv).astype(BF16)

    lane = lax.broadcasted_iota(jnp.int32, (C, SCAN_W), 1) // RWKV_HEAD
    stack = lambda t: jnp.concatenate([jnp.where(lane == hh, t, 0.0) for hh in range(SCAN_HEADS)], axis=0)
    unstack = lambda t: sum(jnp.where(lane == hh, t[hh * C:(hh + 1) * C], 0.0) for hh in range(SCAN_HEADS))
    half = lax.broadcasted_iota(jnp.int32, (C, LANES), 1) // RWKV_HEAD
    swap = lambda t: pltpu.roll(t, RWKV_HEAD, 1)

    def flat_to_local(t):
        tiles = [t[:, 0:LANES], t[:, LANES:2 * LANES]]
        return jnp.concatenate([tiles[hh // 2] if hh % 2 == 0 else swap(tiles[hh // 2])
                                for hh in range(SCAN_HEADS)], axis=0)

    def local_to_flat(t):
        blk = lambda hh: t[hh * C:(hh + 1) * C]
        return jnp.concatenate([jnp.where(half == 0, blk(2 * j), swap(blk(2 * j + 1)))
                                for j in range(SCAN_HEADS // 2)], axis=1)

    def local_to_blockdiag(t, slot):
        rows = []
        zero = jnp.zeros((C, LANES), F32)
        for hh in range(SCAN_HEADS):
            p = t[hh * C:(hh + 1) * C]
            if hh % 2 != slot:
                p = swap(p)
            p = jnp.where(half == hh % 2, p, 0.0)
            rows.append(jnp.concatenate([p, zero] if hh < 2 else [zero, p], axis=1))
        return jnp.concatenate(rows, axis=0).astype(BF16)

    lhs = jnp.concatenate([stack(at), stack(rt)], axis=0).astype(BF16)
    prod = _dot_nt(lhs, jnp.concatenate([kt, bt], axis=0))
    yield
    tt = lax.broadcasted_iota(jnp.int32, (n, LANES), 0) % C
    ss = lax.broadcasted_iota(jnp.int32, (n, LANES), 1) % C
    a_loc = jnp.where((ss > tt) if reverse else (ss < tt), prod[:n], 0.0)
    r_loc = jnp.where((ss >= tt) if reverse else (ss <= tt), prod[n:], 0.0)
    x0 = _dot_nt(jnp.concatenate([at, rt], axis=0).astype(BF16), S.astype(BF16))
    yield
    a_s0 = x0[:C]
    r_s0 = x0[C:]
    slot0 = lax.broadcasted_iota(jnp.int32, (n, LANES), 1) < RWKV_HEAD
    akv = _dot(local_to_blockdiag(a_loc, 0), flat_to_local(v).astype(BF16))
    yield
    xa = jnp.where(slot0, flat_to_local(a_s0) + akv, a_loc)
    ap = local_to_blockdiag(a_loc, 1)
    nsteps = int(math.log2(C))
    for it in range(nsteps):
        step = _dot(ap, xa.astype(BF16))
        yield
        xa = jnp.where(slot0, xa + step, step)
        if it + 1 < nsteps:
            ap = local_to_blockdiag(step, 1)
    u = local_to_flat(xa)
    y_s = _dot(r_loc.astype(BF16), jnp.concatenate([v, u], axis=0).astype(BF16))
    yield
    y = r_s0 + unstack(y_s)
    uv = jnp.concatenate([u, v], axis=0)
    bk = jnp.concatenate([b * pend, kz * pend], axis=0)
    upd = _dot(uv.T.astype(BF16), bk.astype(BF16))
    yield
    vi = lax.broadcasted_iota(jnp.int32, (SCAN_W, SCAN_W), 0) // RWKV_HEAD
    ki = lax.broadcasted_iota(jnp.int32, (SCAN_W, SCAN_W), 1) // RWKV_HEAD
    s_ref[...] = S * jnp.exp(tot) + jnp.where(vi == ki, upd, 0.0)
    return y


SCAN_GROUPS = 4


def _scan_kernel(rf, vf, knf, lwf, kzf, azf, rb, vb, knb, lwb, kzb, azb, yf_o, yb_o, s_ref):
    @pl.when(pl.program_id(2) == 0)
    def _():
        s_ref[...] = jnp.zeros_like(s_ref)

    chains = []
    for g in range(SCAN_GROUPS):
        cs = slice(g * SCAN_W, (g + 1) * SCAN_W)
        chains.append((_scan_chunk((rf, vf, knf, lwf, kzf, azf), cs, s_ref.at[0, g], False), yf_o, cs))
        chains.append((_scan_chunk((rb, vb, knb, lwb, kzb, azb), cs, s_ref.at[1, g], True), yb_o, cs))
    while chains:
        for chain in list(chains):
            gen, out_ref, cs = chain
            try:
                next(gen)
            except StopIteration as done:
                out_ref[:, cs] = done.value
                chains.remove(chain)


def _rwkv_scan(r, v, kn, lw, kz, az, B, S):
    T, D = r.shape
    C = SCAN_C
    nc = S // C
    W = SCAN_W * SCAN_GROUPS
    ng = D // W
    fwd = lambda b, g, c: (b * nc + c, g)
    bwd = lambda b, g, c: (b * nc + nc - 1 - c, g)
    fwd3 = lambda z: (lambda b, g, c: (z, b * nc + c, g))
    bwd3 = lambda z: (lambda b, g, c: (z, b * nc + nc - 1 - c, g))
    s2 = lambda im: pl.BlockSpec((C, W), im)
    s3 = lambda im: pl.BlockSpec((None, C, W), im)
    return pl.pallas_call(
        _scan_kernel,
        grid=(B, ng, nc),
        in_specs=[s2(fwd), s2(fwd), s2(fwd), s3(fwd3(0)), s3(fwd3(0)), s3(fwd3(0)),
                  s2(bwd), s2(bwd), s2(bwd), s3(bwd3(1)), s3(bwd3(1)), s3(bwd3(1))],
        out_specs=[s2(fwd), s2(bwd)],
        out_shape=[jax.ShapeDtypeStruct((T, D), F32)] * 2,
        scratch_shapes=[pltpu.VMEM((2, SCAN_GROUPS, SCAN_W, SCAN_W), F32)],
        compiler_params=_cparams(("parallel", "parallel", "arbitrary")),
        name="l1_wkv_scan",
    )(r, v, kn, lw, kz, az, r, v, kn, lw, kz, az)


def _rwkv_out_kernel(yf_ref, yb_ref, bon_ref, g_ref, lg_ref, lb_ref, bd_ref, o_ref):
    y = yf_ref[...] + yb_ref[...]
    bd = bd_ref[...]
    mu = _segsum(y, bd) * (1.0 / RWKV_HEAD)
    yc = y - mu
    var = _segsum(yc * yc, bd) * (1.0 / RWKV_HEAD)
    yn = yc * lax.rsqrt(var + GN_EPS) * lg_ref[...] + lb_ref[...]
    o_ref[...] = ((yn + bon_ref[...]) * g_ref[...]).astype(BF16)


def _rwkv_out(yf, yb, bon, g, ln_g, ln_b):
    T, D = yf.shape
    tm = 256
    bd = jnp.kron(jnp.eye(MXU_DIM // RWKV_HEAD, dtype=F32), jnp.ones((RWKV_HEAD, RWKV_HEAD), F32)).astype(BF16)
    row = pl.BlockSpec((tm, D), lambda i: (i, 0))
    const = lambda shp: pl.BlockSpec(shp, lambda i: (0,) * len(shp))
    return pl.pallas_call(
        _rwkv_out_kernel,
        grid=(T // tm,),
        in_specs=[row, row, row, row, const((1, D)), const((1, D)), const((MXU_DIM, MXU_DIM))],
        out_specs=row,
        out_shape=jax.ShapeDtypeStruct((T, D), BF16),
        compiler_params=_cparams(("parallel",)),
        name="l1_rwkv_norm_gate",
    )(yf, yb, bon, g, ln_g.reshape(1, D), ln_b.reshape(1, D), bd)


def _peer_layer(x2d, y2d, w_o, mod_l, norm_g2, w_q, subkeys, u_tab, v_tab, B, S):
    x1, ht, st = _post(x2d, y2d, w_o, mod_l, norm_g2, w_q, subkeys, B, S)
    r1, g1, n0, f0 = _topk(st)
    return _peer_dense(x1, mod_l, ht, u_tab, v_tab, r1, g1, n0, f0, B, S)


def kernel(x, c, ada_w, ada_b, norm_g, attn_w_in, attn_qk_g, diff_lambda, diff_subln_g, attn_w_out, rel_bias,
           rwkv_mix, rwkv_w_rkv, rwkv_w0, rwkv_w1, rwkv_w2, rwkv_a0, rwkv_a1, rwkv_a2, rwkv_g1, rwkv_g2,
           rwkv_k_k, rwkv_k_a, rwkv_r_k, rwkv_ln_g, rwkv_ln_b, rwkv_w_o,
           peer_w_q, peer_subkeys, peer_u, peer_v):
    B, S, D = x.shape
    mod = _ada(c, ada_w, ada_b)
    x2d = x.reshape(B * S, D)
    for layer in range(DEPTH):
        j = layer // 2
        mod_l = mod[layer]
        if layer % 2 == 0:
            lambda_init = 0.8 - 0.6 * math.exp(-0.3 * layer)
            qa, ka, va, qb, kb, vb = _inproj(x2d, mod_l, norm_g[layer, 0], attn_w_in[j], attn_qk_g[j], B, S)
            oa = _attn_a(qa, ka, va, diff_lambda[j], diff_subln_g[j], rel_bias, lambda_init, B, S)
            ob = _attn_b(qb, kb, vb, B, S)
            y2d = jnp.concatenate([oa, ob], axis=1)
            w_o = attn_w_out[j]
        else:
            r, v, g, kn, lw, kz, az, bon = _rwkv_pre(
                x2d, mod_l, norm_g[layer, 0], rwkv_mix[j], rwkv_w_rkv[j], rwkv_w0[j], rwkv_w1[j], rwkv_w2[j],
                rwkv_a0[j], rwkv_a1[j], rwkv_a2[j], rwkv_g1[j], rwkv_g2[j], rwkv_k_k[j], rwkv_k_a[j], rwkv_r_k[j], B, S)
            yf, yb = _rwkv_scan(r, v, kn, lw, kz, az, B, S)
            y2d = _rwkv_out(yf, yb, bon, g, rwkv_ln_g[j], rwkv_ln_b[j])
            w_o = rwkv_w_o[j]
        x2d = _peer_layer(x2d, y2d, w_o, mod_l, norm_g[layer, 1], peer_w_q[layer], peer_subkeys[layer],
                          peer_u[layer], peer_v[layer], B, S)
    return x2d.reshape(B, S, D)
```

```python
import functools
import math

import jax
import jax.numpy as jnp
from jax import lax
from jax.experimental import pallas as pl
from jax.experimental.pallas import tpu as pltpu

F32 = jnp.float32
BF16 = jnp.bfloat16
HIGHEST = lax.Precision.HIGHEST

D_MODEL = 1024
DEPTH = 2
HEAD_DIM = 64
A_HEADS = 4
A_MAPS = 8
B_Q_HEADS = 8
B_KV_HEADS = 2
GRID_W = 64
ROPE_AXIS_DIM = 32
ROPE_THETA = 10000.0
REL_BUCKETS = 32
REL_MAX_DIST = 128
RWKV_HEAD = 64
GN_EPS = 64e-5
PEER_HEADS = 8
N_KEYS = 128
PEER_TOPK = 16
NORM_EPS = 1e-6

LANES = 128
MXU_DIM = 256
VMEM_LIMIT = 56 * 1024 * 1024

NT_DIMS = (((1,), (1,)), ((), ()))
LOG2E = math.log2(math.e)


def _cparams(sem):
    return pltpu.CompilerParams(dimension_semantics=sem, vmem_limit_bytes=VMEM_LIMIT)


def _dot(a, b):
    return jnp.dot(a, b, preferred_element_type=F32)


def _dot_nt(a, b):
    return lax.dot_general(a, b, NT_DIMS, preferred_element_type=F32)


def _segsum(v, bd):
    outs = []
    n = v.shape[1]
    for c0 in range(0, n, MXU_DIM):
        w = min(MXU_DIM, n - c0)
        blk = v[:, c0:c0 + w]
        hi = blk.astype(BF16)
        lo = (blk - hi.astype(F32)).astype(BF16)
        b = bd[:w, :w]
        outs.append(_dot(hi, b) + _dot(lo, b))
    return outs[0] if len(outs) == 1 else jnp.concatenate(outs, axis=1)


def _modnorm(x, g, sh, sc):
    ms = jnp.mean(x * x, axis=-1, keepdims=True)
    return (x * lax.rsqrt(ms + NORM_EPS) * g) * (1.0 + sc) + sh


def _ada_kernel(c_ref, w_ref, b_ref, o_ref):
    c = c_ref[...]
    cs = c * (1.0 / (1.0 + jnp.exp(-c)))
    o_ref[...] = jnp.dot(cs, w_ref[...], preferred_element_type=F32, precision=HIGHEST) + b_ref[...]


def _ada(c, ada_w, ada_b):
    B, D = c.shape
    n6 = ada_w.shape[-1]
    tn = 1536
    out = pl.pallas_call(
        _ada_kernel,
        grid=(DEPTH, n6 // tn),
        in_specs=[
            pl.BlockSpec((B, D), lambda l, j: (0, 0)),
            pl.BlockSpec((None, D, tn), lambda l, j: (l, 0, j)),
            pl.BlockSpec((None, 1, tn), lambda l, j: (l, 0, j)),
        ],
        out_specs=pl.BlockSpec((None, B, tn), lambda l, j: (l, 0, j)),
        out_shape=jax.ShapeDtypeStruct((DEPTH, B, n6), F32),
        compiler_params=_cparams(("parallel", "parallel")),
        name="ada_mod",
    )(c, ada_w, ada_b.reshape(DEPTH, 1, n6))
    return out.reshape(DEPTH, B, 6, D)


C_QA, C_KA, C_VA, C_QB, C_KB, C_VB, C_QBS, C_KBS, C_END = 0, 512, 1024, 1536, 2048, 2304, 2560, 3072, 3328


def _inproj_kernel(x_ref, mod_ref, g_ref, w_ref, gain_ref, cos_ref, sin_ref, bd_ref,
                   qa_ref, ka_ref, va_ref, qb_ref, kb_ref, vb_ref):
    h = _modnorm(x_ref[...], g_ref[...], mod_ref[0:1, :], mod_ref[1:2, :])
    y = _dot(h.astype(BF16), w_ref[...])
    bd = bd_ref[...]
    gain = gain_ref[...]

    def rinv(sec):
        return lax.rsqrt(_segsum(sec * sec, bd) * (1.0 / HEAD_DIM) + NORM_EPS)

    yqa = y[:, C_QA:C_KA]
    qa_ref[...] = (yqa * rinv(yqa) * gain[:, C_QA:C_KA]).astype(BF16)
    yka = y[:, C_KA:C_VA]
    ka_ref[...] = (yka * rinv(yka) * gain[:, C_KA:C_VA]).astype(BF16)
    va_ref[...] = y[:, C_VA:C_QB].astype(BF16)
    cos = cos_ref[...]
    sin = sin_ref[...]
    yqb = y[:, C_QB:C_KB]
    qb = (yqb * gain[:, C_QB:C_KB] * cos + y[:, C_QBS:C_KBS] * gain[:, C_QBS:C_KBS] * sin) * rinv(yqb)
    qb_ref[...] = qb.astype(BF16)
    ykb = y[:, C_KB:C_VB]
    kb = (ykb * gain[:, C_KB:C_VB] * cos[:, :256] + y[:, C_KBS:C_END] * gain[:, C_KBS:C_END] * sin[:, :256]) * rinv(ykb)
    kb_ref[...] = kb.astype(BF16)
    vb_ref[...] = y[:, C_VB:C_QBS].astype(BF16)


def _swap16(a):
    sh = a.shape
    a = a.reshape(sh[:-1] + (sh[-1] // 32, 2, 16))
    return a[..., ::-1, :].reshape(sh)


def _inproj(x2d, mod_l, norm_g, w_in, qk_g, B, S):
    T, D = x2d.shape
    tm = 256
    nper = S // tm
    scale = HEAD_DIM ** -0.5 * LOG2E
    wqa, wka, wva, wqb, wkb, wvb = (w_in[:, a:b] for a, b in
                                    ((0, 512), (512, 1024), (1024, 1536), (1536, 2048), (2048, 2176), (2176, 2304)))
    dup = lambda w: jnp.repeat(w.reshape(D, B_KV_HEADS, 1, HEAD_DIM), 2, axis=2).reshape(D, 4 * HEAD_DIM)
    wkbd = dup(wkb)
    w_all = jnp.concatenate([wqa, wka, wva, wqb, wkbd, dup(wvb), _swap16(wqb), _swap16(wkbd)], axis=1).astype(BF16)
    gq = jnp.tile(qk_g[2], B_Q_HEADS) * scale
    gk = jnp.tile(qk_g[3], 4)
    gain = jnp.concatenate([jnp.tile(qk_g[0], A_MAPS) * scale, jnp.tile(qk_g[1], A_MAPS), jnp.ones((512,), F32),
                            gq, gk, jnp.ones((256,), F32), _swap16(gq), _swap16(gk)]).reshape(1, C_END)
    pos = jnp.arange(S)
    row = (pos // GRID_W).astype(F32)
    col = (pos % GRID_W).astype(F32)
    inv = ROPE_THETA ** (-jnp.arange(0, ROPE_AXIS_DIM, 2, dtype=F32) / ROPE_AXIS_DIM)
    ang = jnp.concatenate([row[:, None] * inv] * 2 + [col[:, None] * inv] * 2, axis=1)
    sign = jnp.tile(jnp.concatenate([-jnp.ones((16,), F32), jnp.ones((16,), F32)]), 2)
    cos_t = jnp.tile(jnp.cos(ang), (1, B_Q_HEADS))
    sin_t = jnp.tile(jnp.sin(ang) * sign, (1, B_Q_HEADS))
    bd = jnp.kron(jnp.eye(MXU_DIM // HEAD_DIM, dtype=F32), jnp.ones((HEAD_DIM, HEAD_DIM), F32)).astype(BF16)
    row_spec = lambda w: pl.BlockSpec((tm, w), lambda i: (i, 0))
    const = lambda shp: pl.BlockSpec(shp, lambda i: (0,) * len(shp))
    outs = pl.pallas_call(
        _inproj_kernel,
        grid=(T // tm,),
        in_specs=[
            row_spec(D),
            pl.BlockSpec((None, 6, D), lambda i: (i // nper, 0, 0)),
            const((1, D)),
            const((D, C_END)),
            const((1, C_END)),
            pl.BlockSpec((tm, 512), lambda i: (i % nper, 0)),
            pl.BlockSpec((tm, 512), lambda i: (i % nper, 0)),
            const((MXU_DIM, MXU_DIM)),
        ],
        out_specs=[row_spec(512), row_spec(512), row_spec(512), row_spec(512), row_spec(256), row_spec(256)],
        out_shape=[jax.ShapeDtypeStruct((T, w), BF16) for w in (512, 512, 512, 512, 256, 256)],
        compiler_params=_cparams(("parallel",)),
        name="l0_inproj",
    )(x2d, mod_l, norm_g.reshape(1, D), w_all, gain, cos_t, sin_t, bd)
    return outs


def _t5_bucket(rel):
    nb = REL_BUCKETS // 2
    max_exact = nb // 2
    ret = jnp.where(rel > 0, nb, 0)
    n = jnp.abs(rel)
    nf = jnp.maximum(n, 1).astype(F32)
    large = max_exact + (jnp.log(nf / max_exact) / math.log(REL_MAX_DIST / max_exact) * (nb - max_exact)).astype(jnp.int32)
    large = jnp.minimum(large, nb - 1)
    return ret + jnp.where(n < max_exact, n, large)


def _softmax_parts(s):
    m = jnp.max(s, axis=1, keepdims=True)
    e = jnp.exp2(s - m)
    return e, 1.0 / jnp.sum(e, axis=1, keepdims=True)


def _attn_a_kernel(lam_ref, q_ref, k_ref, v_ref, bias_ref, sg_ref, o_ref, *, tq, nsub, S, lambda_init):
    i = pl.program_id(2)
    lv = lam_ref[...]
    lam = (jnp.exp(jnp.sum(lv[0:1] * lv[1:2], axis=1, keepdims=True))
           - jnp.exp(jnp.sum(lv[2:3] * lv[3:4], axis=1, keepdims=True)) + lambda_init)
    k = k_ref[...]
    v = v_ref[...]
    lane = lax.broadcasted_iota(jnp.int32, (tq, LANES), 1)
    nsl = tq // LANES + 4

    def scores(sub):
        q = q_ref[sub * tq:(sub + 1) * tq, :]
        return [_dot_nt(jnp.where((lane < HEAD_DIM) == (mi == 0), q, jnp.zeros_like(q)), k) for mi in range(2)]

    s_next = scores(0)
    for sub in range(nsub):
        ss = s_next
        if sub + 1 < nsub:
            s_next = scores(sub + 1)
        blk = i * nsub + sub
        ps = []
        for mi in range(2):
            tiles = []
            for j in range(S // LANES):
                idx = jnp.clip(j - blk * (tq // LANES) + 2, 0, nsl - 1)
                tiles.append(bias_ref[mi, idx])
            ps.append(_softmax_parts(ss[mi] + jnp.concatenate(tiles, axis=1)))
        diff = (ps[0][0] * ps[0][1] - ps[1][0] * (lam * ps[1][1])).astype(BF16)
        oa = _dot(diff, v)
        ms = jnp.mean(oa * oa, axis=1, keepdims=True)
        o_ref[sub * tq:(sub + 1) * tq, :] = (oa * lax.rsqrt(ms + NORM_EPS) * sg_ref[...]
                                             * (1.0 - lambda_init)).astype(BF16)


def _attn_a(qa, ka, va, diff_lambda, subln_g, rel_bias, lambda_init, B, S):
    tq = 256
    nsub = 2 if S % (2 * tq) == 0 else 1
    tqb = tq * nsub
    nq = S // tqb
    nsl = tq // LANES + 4
    d = jnp.arange(nsl) - 2
    rel = d[:, None, None] * LANES + jnp.arange(LANES)[None, None, :] - jnp.arange(tq)[None, :, None]
    rel = jnp.where(d[:, None, None] < -1, -S, jnp.where(d[:, None, None] > tq // LANES, S, rel))
    onehot = (_t5_bucket(rel)[None] == jnp.arange(REL_BUCKETS)[:, None, None, None]).astype(F32)
    bias_tab = jnp.einsum("bm,bsqc->msqc", rel_bias.astype(F32) * LOG2E, onehot, precision=HIGHEST)
    kern = functools.partial(_attn_a_kernel, tq=tq, nsub=nsub, S=S, lambda_init=lambda_init)
    return pl.pallas_call(
        kern,
        grid=(B, A_HEADS, nq),
        in_specs=[
            pl.BlockSpec((4, HEAD_DIM), lambda b, h, i: (0, 0)),
            pl.BlockSpec((tqb, LANES), lambda b, h, i: (b * nq + i, h)),
            pl.BlockSpec((S, LANES), lambda b, h, i: (b, h)),
            pl.BlockSpec((S, LANES), lambda b, h, i: (b, h)),
            pl.BlockSpec((2, nsl, tq, LANES), lambda b, h, i: (h, 0, 0, 0)),
            pl.BlockSpec((1, LANES), lambda b, h, i: (0, 0)),
        ],
        out_specs=pl.BlockSpec((tqb, LANES), lambda b, h, i: (b * nq + i, h)),
        out_shape=jax.ShapeDtypeStruct((B * S, A_HEADS * LANES), BF16),
        compiler_params=_cparams(("parallel", "parallel", "arbitrary")),
        name="l0_attn_diff",
    )(diff_lambda, qa, ka, va, bias_tab, subln_g.reshape(1, LANES))


def _attn_b_kernel(q_ref, k_ref, v_ref, o_ref):
    q = q_ref[...]
    k = k_ref[...]
    v = v_ref[...]
    lane = lax.broadcasted_iota(jnp.int32, (q.shape[0], LANES), 1)

    def scores(r):
        qp = q[:, (r // 2) * LANES:(r // 2 + 1) * LANES]
        qm = jnp.where((lane < HEAD_DIM) == (r % 2 == 0), qp, jnp.zeros_like(qp))
        return _dot_nt(qm, k)

    nh = B_Q_HEADS // B_KV_HEADS
    heads = []
    s_next = scores(0)
    for r in range(nh):
        s = s_next
        if r + 1 < nh:
            s_next = scores(r + 1)
        e, rinv = _softmax_parts(s)
        heads.append(_dot(e.astype(BF16), v) * rinv)
    outs = [jnp.where(lane < HEAD_DIM, heads[2 * u], heads[2 * u + 1]) for u in range(nh // 2)]
    o_ref[...] = jnp.concatenate(outs, axis=1).astype(BF16)


def _attn_b(qb, kb, vb, B, S):
    tq = 256
    nq = S // tq
    return pl.pallas_call(
        _attn_b_kernel,
        grid=(B, B_KV_HEADS, nq),
        in_specs=[
            pl.BlockSpec((tq, 2 * LANES), lambda b, g, i: (b * nq + i, g)),
            pl.BlockSpec((S, LANES), lambda b, g, i: (b, g)),
            pl.BlockSpec((S, LANES), lambda b, g, i: (b, g)),
        ],
        out_specs=pl.BlockSpec((tq, 2 * LANES), lambda b, g, i: (b * nq + i, g)),
        out_shape=jax.ShapeDtypeStruct((B * S, B_Q_HEADS * HEAD_DIM), BF16),
        compiler_params=_cparams(("parallel", "parallel", "arbitrary")),
        name="l0_attn_gqa",
    )(qb, kb, vb)


def _post_kernel(x_ref, ya_ref, yb_ref, wa_ref, wb_ref, mod_ref, g_ref, wq_ref, sk_ref, x1_ref, ht_ref, st_ref):
    y = _dot(ya_ref[...], wa_ref[...]) + _dot(yb_ref[...], wb_ref[...])
    x1 = x_ref[...] + mod_ref[2:3, :] * y
    x1_ref[...] = x1
    h2 = _modnorm(x1, g_ref[...], mod_ref[3:4, :], mod_ref[4:5, :])
    ht_ref[...] = h2.T.astype(BF16)
    q = _dot(h2.astype(BF16), wq_ref[...])
    for hp in range(2 * PEER_HEADS):
        qs = q[:, hp * N_KEYS:(hp + 1) * N_KEYS].astype(BF16)
        st_ref[hp] = _dot_nt(sk_ref[hp], qs)


def _post(x2d, y_halves, w_o, mod_l, norm_g, w_q, subkeys, B, S):
    T, D = x2d.shape
    tm = 256
    nper = S // tm
    (ya, ca), (yb, cb) = y_halves
    half = w_o.shape[0] // 2
    nq = w_q.shape[1]
    w_o = w_o.astype(BF16)
    sk = subkeys.reshape(2 * PEER_HEADS, N_KEYS, N_KEYS).astype(BF16)
    const = lambda shp: pl.BlockSpec(shp, lambda i: (0,) * len(shp))
    return pl.pallas_call(
        _post_kernel,
        grid=(T // tm,),
        in_specs=[
            pl.BlockSpec((tm, D), lambda i: (i, 0)),
            pl.BlockSpec((tm, half), lambda i: (i, ca)),
            pl.BlockSpec((tm, half), lambda i: (i, cb)),
            const((half, D)),
            const((half, D)),
            pl.BlockSpec((None, 6, D), lambda i: (i // nper, 0, 0)),
            const((1, D)),
            const((D, nq)),
            const((2 * PEER_HEADS, N_KEYS, N_KEYS)),
        ],
        out_specs=[
            pl.BlockSpec((tm, D), lambda i: (i, 0)),
            pl.BlockSpec((D, tm), lambda i: (0, i)),
            pl.BlockSpec((2 * PEER_HEADS, N_KEYS, tm), lambda i: (0, 0, i)),
        ],
        out_shape=[
            jax.ShapeDtypeStruct((T, D), F32),
            jax.ShapeDtypeStruct((D, T), BF16),
            jax.ShapeDtypeStruct((2 * PEER_HEADS, N_KEYS, T), F32),
        ],
        compiler_params=_cparams(("parallel",)),
        name="mix_out_peer_q",
    )(x2d, ya, yb, w_o[:half], w_o[half:], mod_l, norm_g.reshape(1, D), w_q.astype(BF16), sk)


_STAIR = [(i, j) for i in range(PEER_TOPK) for j in range(PEER_TOPK) if (i + 1) * (j + 1) <= PEER_TOPK]
_STAIR_ROWS = 56
_NEG = float("-inf")
WDT = BF16


def _extract_top(s, iota, nsel, sentinel, exact):
    vals = []
    rank = jnp.full(s.shape, float(nsel), F32)
    for r in range(nsel):
        m = jnp.max(s, axis=0, keepdims=True)
        hit = s == m
        if exact:
            hit = iota == jnp.min(jnp.where(hit, iota, sentinel), axis=0, keepdims=True)
        vals.append(m)
        rank = jnp.where(hit, float(r), rank)
        s = jnp.where(hit, _NEG, s)
    return vals, rank


def _peer_head_select(s0, s1, iota, iota_c, group, exact):
    a, rank0 = _extract_top(s0, iota, PEER_TOPK, float(N_KEYS), exact)
    b, rank1 = _extract_top(s1, iota, PEER_TOPK, float(N_KEYS), exact)
    cand = jnp.full(iota_c.shape, _NEG, F32)
    for p, (i, j) in enumerate(_STAIR):
        cand = jnp.where(iota_c == float(p), a[i] + b[j], cand)
    _, crank = _extract_top(cand, iota_c, PEER_TOPK, float(_STAIR_ROWS), exact)
    sel = jnp.where(crank < float(PEER_TOPK), 1.0, 0.0)
    z = jnp.sum(sel * jnp.exp(cand - (a[0] + b[0])), axis=0, keepdims=True)
    n0 = jnp.zeros(s0.shape, F32)
    for i in range(PEER_TOPK):
        cnt = jnp.sum(jnp.where(group == float(i), sel, 0.0), axis=0, keepdims=True)
        n0 = jnp.where(rank0 == float(i), cnt, n0)
    marked = (jnp.sum(jnp.where(rank0 < float(PEER_TOPK), 1.0, 0.0), axis=0, keepdims=True)
              + jnp.sum(jnp.where(rank1 < float(PEER_TOPK), 1.0, 0.0), axis=0, keepdims=True)
              + jnp.sum(sel, axis=0, keepdims=True))
    clean = jnp.max(jnp.abs(marked - 3.0 * PEER_TOPK)) == 0.0
    return rank1, jnp.exp(s1 - b[0]), n0, jnp.exp(s0 - a[0]) * (1.0 / z), clean


def _topk_kernel(st_ref, r1_ref, g1_ref, n0_ref, f0_ref, *, tt):
    iota = lax.broadcasted_iota(jnp.int32, (N_KEYS, tt), 0).astype(F32)
    iota_c = lax.broadcasted_iota(jnp.int32, (_STAIR_ROWS, tt), 0).astype(F32)
    starts = [p for p, (i, j) in enumerate(_STAIR) if j == 0]
    group = jnp.zeros((_STAIR_ROWS, tt), F32)
    for st in starts[1:]:
        group = group + jnp.where(iota_c >= float(st), 1.0, 0.0)

    def store(h, rank1, g1, n0, f0):
        r1_ref[h] = rank1.astype(WDT)
        g1_ref[h] = g1.astype(WDT)
        n0_ref[h] = n0
        f0_ref[h] = f0

    def head_body(h, carry):
        s0 = st_ref[2 * h]
        s1 = st_ref[2 * h + 1]
        rank1, g1, n0, f0, clean = _peer_head_select(s0, s1, iota, iota_c, group, exact=False)
        store(h, rank1, g1, n0, f0)

        @pl.when(jnp.logical_not(clean))
        def _():
            store(h, *_peer_head_select(s0, s1, iota, iota_c, group, exact=True)[:4])

        return carry

    lax.fori_loop(0, PEER_HEADS, head_body, 0)


def _topk(st):
    T = st.shape[-1]
    tt = 512
    blk = pl.BlockSpec((PEER_HEADS, N_KEYS, tt), lambda i: (0, 0, i))
    return pl.pallas_call(
        functools.partial(_topk_kernel, tt=tt),
        grid=(T // tt,),
        in_specs=[pl.BlockSpec((2 * PEER_HEADS, N_KEYS, tt), lambda i: (0, 0, i))],
        out_specs=[blk, blk, blk, blk],
        out_shape=[jax.ShapeDtypeStruct((PEER_HEADS, N_KEYS, T), WDT)] * 2
                  + [jax.ShapeDtypeStruct((PEER_HEADS, N_KEYS, T), F32)] * 2,
        compiler_params=_cparams(("parallel",)),
        name="peer_topk",
    )(st)


_GELU_K0 = -2.0 * math.sqrt(2.0 / math.pi) * math.log2(math.e)
_GELU_K1 = _GELU_K0 * 0.044715


def _gelu_tanh(x):
    return x / (1.0 + jnp.exp2(x * (_GELU_K0 + _GELU_K1 * (x * x))))


def _peer_dense_kernel(x_ref, mod_ref, ht_ref, u_ref, v_ref, r1_ref, g1_ref, n0_ref, f0_ref, o_ref, acc_ref, z_ref, act_ref,
                       *, tb, eb):
    e = pl.program_id(1)

    @pl.when(e == 0)
    def _():
        acc_ref[...] = jnp.zeros_like(acc_ref)

    nch = eb // MXU_DIM
    chunk = lambda k: slice(k * MXU_DIM, (k + 1) * MXU_DIM)
    def activations(k):
        act_ref[k % 2] = _dot(u_ref[chunk(k), :], ht_ref[...])

    activations(0)
    for k in range(nch):
        if k + 1 < nch:
            activations(k + 1)
        for r in range(MXU_DIM // N_KEYS):
            ii = k * (MXU_DIM // N_KEYS) + r
            for c0 in range(0, tb, MXU_DIM):
                cs = slice(c0, c0 + MXU_DIM)
                w = jnp.zeros((N_KEYS, MXU_DIM), WDT)
                for h in range(PEER_HEADS):
                    n0 = n0_ref[h, ii:ii + 1, cs].astype(WDT)
                    f0 = f0_ref[h, ii:ii + 1, cs].astype(WDT)
                    w = w + jnp.where(r1_ref[h, :, cs] < n0, g1_ref[h, :, cs] * f0, jnp.zeros((), WDT))
                act = act_ref[k % 2, r * N_KEYS:(r + 1) * N_KEYS, cs]
                z_ref[ii * N_KEYS:(ii + 1) * N_KEYS, cs] = w * _gelu_tanh(act).astype(WDT)
        acc_ref[...] += lax.dot_general(z_ref[chunk(k), :], v_ref[chunk(k), :], (((0,), (0,)), ((), ())),
                                        preferred_element_type=F32)

    @pl.when(e == pl.num_programs(1) - 1)
    def _():
        o_ref[...] = x_ref[...] + mod_ref[5:6, :] * acc_ref[...]


def _peer_dense(x2d, mod_l, ht, u_tab, v_tab, r1, g1, n0, f0, B, S):
    T, D = x2d.shape
    E = u_tab.shape[0]
    tb = min(512, S)
    eb = 2048
    nper = S // tb
    tok = pl.BlockSpec((PEER_HEADS, N_KEYS, tb), lambda t, e: (0, 0, t))
    rows = pl.BlockSpec((PEER_HEADS, eb // N_KEYS, tb), lambda t, e: (0, e, t))
    return pl.pallas_call(
        functools.partial(_peer_dense_kernel, tb=tb, eb=eb),
        grid=(T // tb, E // eb),
        in_specs=[
            pl.BlockSpec((tb, D), lambda t, e: (t, 0)),
            pl.BlockSpec((None, 6, D), lambda t, e: (t // nper, 0, 0)),
            pl.BlockSpec((D, tb), lambda t, e: (0, t)),
            pl.BlockSpec((eb, D), lambda t, e: (e, 0)),
            pl.BlockSpec((eb, D), lambda t, e: (e, 0)),
            tok, tok, rows, rows,
        ],
        out_specs=pl.BlockSpec((tb, D), lambda t, e: (t, 0)),
        out_shape=jax.ShapeDtypeStruct((T, D), F32),
        scratch_shapes=[pltpu.VMEM((tb, D), F32), pltpu.VMEM((eb, tb), WDT), pltpu.VMEM((2, MXU_DIM, tb), F32)],
        compiler_params=_cparams(("parallel", "arbitrary")),
        name="peer_dense",
    )(x2d, mod_l, ht, u_tab, v_tab, r1, g1, n0, f0)


def _rwkv_pre_kernel(x_ref, xp_ref, xn_ref, mod_ref, g_ref, mix_ref, wr_ref, wk_ref, wv_ref, g1_ref, g2_ref,
                     w1_ref, w2_ref, a1_ref, a2_ref, w0_ref, a0_ref, kk_ref, ka_ref, rk_ref, bd_ref,
                     r_o, v_o, g_o, kn_o, lw_o, kz_o, az_o, bon_o, *, nper):
    i = pl.program_id(0)
    gn = g_ref[...]
    sh = mod_ref[0:1, :]
    sc = mod_ref[1:2, :]
    h = _modnorm(x_ref[...], gn, sh, sc)
    tm = h.shape[0]
    hp = _modnorm(xp_ref[...], gn, sh, sc)[7:8, :]
    hn = _modnorm(xn_ref[...], gn, sh, sc)[0:1, :]
    hp = jnp.where(i % nper == 0, 0.0, hp)
    hn = jnp.where(i % nper == nper - 1, 0.0, hn)
    rows = lax.broadcasted_iota(jnp.int32, h.shape, 0)
    prev = jnp.where(rows == 0, hp, pltpu.roll(h, 1, 0))
    nxt = jnp.where(rows == tm - 1, hn, pltpu.roll(h, tm - 1, 0))
    xx = 0.5 * (prev + nxt) - h
    mix = mix_ref[...]
    xr, xw, xk, xv, xa, xg = ((h + xx * mix[j:j + 1, :]).astype(BF16) for j in range(6))
    r = _dot(xr, wr_ref[...])
    k = _dot(xk, wk_ref[...])
    v = _dot(xv, wv_ref[...])
    gl = _dot(xg, g1_ref[...])
    g = _dot((1.0 / (1.0 + jnp.exp(-gl))).astype(BF16), g2_ref[...])
    wl = _dot(jnp.tanh(_dot(xw, w1_ref[...])).astype(BF16), w2_ref[...])
    al = _dot(_dot(xa, a1_ref[...]).astype(BF16), a2_ref[...])
    bd = bd_ref[...]
    kraw = k * kk_ref[...]
    kn = kraw / jnp.maximum(jnp.sqrt(_segsum(kraw * kraw, bd)), 1e-12)
    r_o[...] = r
    v_o[...] = v
    g_o[...] = g
    kn_o[...] = kn
    D = r.shape[1]
    bon = jnp.zeros_like(r)
    for z in range(2):
        u = -(w0_ref[z:z + 1, :] + wl[:, z * D:(z + 1) * D])
        softplus = jnp.maximum(u, 0.0) + jnp.log(1.0 + jnp.exp(-jnp.abs(u)))
        lw_o[z] = -jnp.exp(-softplus - 0.5)
        a = 1.0 / (1.0 + jnp.exp(-(a0_ref[z:z + 1, :] + al[:, z * D:(z + 1) * D])))
        kz = k * (1.0 + (a - 1.0) * ka_ref[...])
        kz_o[z] = kz
        az_o[z] = a
        bon = bon + _segsum(r * kz * rk_ref[...], bd) * v
    bon_o[...] = bon


def _blockdiag2(w):
    n, D = w.shape[1], w.shape[2]
    z = jnp.zeros((n, D), w.dtype)
    return jnp.concatenate([jnp.concatenate([w[0], z], axis=1), jnp.concatenate([z, w[1]], axis=1)], axis=0)


def _rwkv_pre(x2d, mod_l, norm_g, mix, w_rkv, w0, w1, w2, a0, a1, a2, g1, g2, k_k, k_a, r_k, B, S):
    T, D = x2d.shape
    tm = 256
    nper = S // tm
    nblk8 = T // 8
    bd = jnp.kron(jnp.eye(MXU_DIM // RWKV_HEAD, dtype=F32), jnp.ones((RWKV_HEAD, RWKV_HEAD), F32)).astype(BF16)
    w1c = jnp.concatenate([w1[0], w1[1]], axis=1).astype(BF16)
    a1c = jnp.concatenate([a1[0], a1[1]], axis=1).astype(BF16)
    w2b = _blockdiag2(w2).astype(BF16)
    a2b = _blockdiag2(a2).astype(BF16)
    const = lambda shp: pl.BlockSpec(shp, lambda i: (0,) * len(shp), pipeline_mode=pl.Buffered(1))
    row = pl.BlockSpec((tm, D), lambda i: (i, 0))
    row2 = pl.BlockSpec((2, tm, D), lambda i: (0, i, 0))
    return pl.pallas_call(
        functools.partial(_rwkv_pre_kernel, nper=nper),
        grid=(T // tm,),
        in_specs=[
            row,
            pl.BlockSpec((8, D), lambda i: (jnp.maximum(i * (tm // 8) - 1, 0), 0)),
            pl.BlockSpec((8, D), lambda i: (jnp.minimum((i + 1) * (tm // 8), nblk8 - 1), 0)),
            pl.BlockSpec((None, 6, D), lambda i: (i // nper, 0, 0)),
            const((1, D)), const((6, D)),
            const((D, D)), const((D, D)), const((D, D)),
            const(g1.shape), const(g2.shape),
            const(w1c.shape), const(w2b.shape), const(a1c.shape), const(a2b.shape),
            const((2, D)), const((2, D)), const((1, D)), const((1, D)), const((1, D)),
            const((MXU_DIM, MXU_DIM)),
        ],
        out_specs=[row, row, row, row, row2, row2, row2, row],
        out_shape=[jax.ShapeDtypeStruct((T, D), F32)] * 4 + [jax.ShapeDtypeStruct((2, T, D), F32)] * 3
                  + [jax.ShapeDtypeStruct((T, D), F32)],
        compiler_params=_cparams(("parallel",)),
        name="l1_rwkv_proj",
    )(x2d, x2d, x2d, mod_l, norm_g.reshape(1, D), mix,
      w_rkv[0].astype(BF16), w_rkv[1].astype(BF16), w_rkv[2].astype(BF16), g1.astype(BF16), g2.astype(BF16),
      w1c, w2b, a1c, a2b, w0, a0, k_k.reshape(1, D), k_a.reshape(1, D), r_k.reshape(1, D), bd)


SCAN_C = 64
SCAN_W = 256
SCAN_HEADS = SCAN_W // RWKV_HEAD


def _scan_chunk(refs, cs, s_ref, reverse):
    r, v, kn, lw, kz, az = (ref[:, cs] for ref in refs)
    S = s_ref[...]
    C = SCAN_C
    n = SCAN_HEADS * C
    ti = lax.broadcasted_iota(jnp.int32, (C, C), 0)
    si = lax.broadcasted_iota(jnp.int32, (C, C), 1)
    inc = jnp.where((si >= ti) if reverse else (si <= ti), 1.0, 0.0).astype(BF16)
    lw_hi = lw.astype(BF16)
    lw_lo = (lw - lw_hi.astype(F32)).astype(BF16)
    cum = _dot(inc, lw_hi) + _dot(inc, lw_lo)
    yield
    tot = cum[0:1, :] if reverse else cum[C - 1:C, :]
    a = -kn
    b = kn * az
    pinv = jnp.exp(-cum)
    pend = jnp.exp(tot - cum)
    rt = r * jnp.exp(cum)
    at = a * jnp.exp(cum - lw)
    bt = (b * pinv).astype(BF16)
    kt = (kz * pinv).astype(BF16)

    lane = lax.broadcasted_iota(jnp.int32, (C, SCAN_W), 1) // RWKV_HEAD
    stack = lambda t: jnp.concatenate([jnp.where(lane == hh, t, 0.0) for hh in range(SCAN_HEADS)], axis=0)
    unstack = lambda t: sum(jnp.where(lane == hh, t[hh * C:(hh + 1) * C], 0.0) for hh in range(SCAN_HEADS))
    half = lax.broadcasted_iota(jnp.int32, (C, LANES), 1) // RWKV_HEAD
    swap = lambda t: pltpu.roll(t, RWKV_HEAD, 1)

    def flat_to_local(t):
        tiles = [t[:, 0:LANES], t[:, LANES:2 * LANES]]
        return jnp.concatenate([tiles[hh // 2] if hh % 2 == 0 else swap(tiles[hh // 2])
                                for hh in range(SCAN_HEADS)], axis=0)

    def local_to_flat(t):
        blk = lambda hh: t[hh * C:(hh + 1) * C]
        return jnp.concatenate([jnp.where(half == 0, blk(2 * j), swap(blk(2 * j + 1)))
                                for j in range(SCAN_HEADS // 2)], axis=1)

    def local_to_blockdiag(t, slot):
        rows = []
        zero = jnp.zeros((C, LANES), F32)
        for hh in range(SCAN_HEADS):
            p = t[hh * C:(hh + 1) * C]
            if hh % 2 != slot:
                p = swap(p)
            p = jnp.where(half == hh % 2, p, 0.0)
            rows.append(jnp.concatenate([p, zero] if hh < 2 else [zero, p], axis=1))
        return jnp.concatenate(rows, axis=0).astype(BF16)

    lhs = jnp.concatenate([stack(at), stack(rt)], axis=0).astype(BF16)
    prod = _dot_nt(lhs, jnp.concatenate([kt, bt], axis=0))
    yield
    tt = lax.broadcasted_iota(jnp.int32, (n, LANES), 0) % C
    ss = lax.broadcasted_iota(jnp.int32, (n, LANES), 1) % C
    a_loc = jnp.where((ss > tt) if reverse else (ss < tt), prod[:n], 0.0)
    r_loc = jnp.where((ss >= tt) if reverse else (ss <= tt), prod[n:], 0.0)
    x0 = _dot_nt(jnp.concatenate([at, rt], axis=0).astype(BF16), S.astype(BF16))
    yield
    a_s0 = x0[:C]
    r_s0 = x0[C:]
    slot0 = lax.broadcasted_iota(jnp.int32, (n, LANES), 1) < RWKV_HEAD
    akv = _dot(local_to_blockdiag(a_loc, 0), flat_to_local(v).astype(BF16))
    yield
    xa = jnp.where(slot0, flat_to_local(a_s0) + akv, a_loc)
    ap = local_to_blockdiag(a_loc, 1)
    nsteps = int(math.log2(C))
    for it in range(nsteps):
        step = _dot(ap, xa.astype(BF16))
        yield
        xa = jnp.where(slot0, xa + step, step)
        if it + 1 < nsteps:
            ap = local_to_blockdiag(step, 1)
    u = local_to_flat(xa)
    y_s = _dot(r_loc.astype(BF16), jnp.concatenate([v, u], axis=0).astype(BF16))
    yield
    y = r_s0 + unstack(y_s)
    uv = jnp.concatenate([u, v], axis=0)
    bk = jnp.concatenate([b * pend, kz * pend], axis=0)
    upd = _dot(uv.T.astype(BF16), bk.astype(BF16))
    yield
    vi = lax.broadcasted_iota(jnp.int32, (SCAN_W, SCAN_W), 0) // RWKV_HEAD
    ki = lax.broadcasted_iota(jnp.int32, (SCAN_W, SCAN_W), 1) // RWKV_HEAD
    s_ref[...] = S * jnp.exp(tot) + jnp.where(vi == ki, upd, 0.0)
    return y


SCAN_GROUPS = 4


def _scan_kernel(rf, vf, knf, lwf, kzf, azf, rb, vb, knb, lwb, kzb, azb, yf_o, yb_o, s_ref):
    @pl.when(pl.program_id(2) == 0)
    def _():
        s_ref[...] = jnp.zeros_like(s_ref)

    chains = []
    for g in range(SCAN_GROUPS):
        cs = slice(g * SCAN_W, (g + 1) * SCAN_W)
        chains.append((_scan_chunk((rf, vf, knf, lwf, kzf, azf), cs, s_ref.at[0, g], False), yf_o, cs))
        chains.append((_scan_chunk((rb, vb, knb, lwb, kzb, azb), cs, s_ref.at[1, g], True), yb_o, cs))
    while chains:
        for chain in list(chains):
            gen, out_ref, cs = chain
            try:
                next(gen)
            except StopIteration as done:
                out_ref[:, cs] = done.value
                chains.remove(chain)


def _rwkv_scan(r, v, kn, lw, kz, az, B, S):
    T, D = r.shape
    C = SCAN_C
    nc = S // C
    W = SCAN_W * SCAN_GROUPS
    ng = D // W
    fwd = lambda b, g, c: (b * nc + c, g)
    bwd = lambda b, g, c: (b * nc + nc - 1 - c, g)
    fwd3 = lambda z: (lambda b, g, c: (z, b * nc + c, g))
    bwd3 = lambda z: (lambda b, g, c: (z, b * nc + nc - 1 - c, g))
    s2 = lambda im: pl.BlockSpec((C, W), im)
    s3 = lambda im: pl.BlockSpec((None, C, W), im)
    return pl.pallas_call(
        _scan_kernel,
        grid=(B, ng, nc),
        in_specs=[s2(fwd), s2(fwd), s2(fwd), s3(fwd3(0)), s3(fwd3(0)), s3(fwd3(0)),
                  s2(bwd), s2(bwd), s2(bwd), s3(bwd3(1)), s3(bwd3(1)), s3(bwd3(1))],
        out_specs=[s2(fwd), s2(bwd)],
        out_shape=[jax.ShapeDtypeStruct((T, D), F32)] * 2,
        scratch_shapes=[pltpu.VMEM((2, SCAN_GROUPS, SCAN_W, SCAN_W), F32)],
        compiler_params=_cparams(("parallel", "parallel", "arbitrary")),
        name="l1_wkv_scan",
    )(r, v, kn, lw, kz, az, r, v, kn, lw, kz, az)


def _rwkv_out_kernel(yf_ref, yb_ref, bon_ref, g_ref, lg_ref, lb_ref, bd_ref, o_ref):
    y = yf_ref[...] + yb_ref[...]
    bd = bd_ref[...]
    mu = _segsum(y, bd) * (1.0 / RWKV_HEAD)
    yc = y - mu
    var = _segsum(yc * yc, bd) * (1.0 / RWKV_HEAD)
    yn = yc * lax.rsqrt(var + GN_EPS) * lg_ref[...] + lb_ref[...]
    o_ref[...] = ((yn + bon_ref[...]) * g_ref[...]).astype(BF16)


def _rwkv_out(yf, yb, bon, g, ln_g, ln_b):
    T, D = yf.shape
    tm = 256
    bd = jnp.kron(jnp.eye(MXU_DIM // RWKV_HEAD, dtype=F32), jnp.ones((RWKV_HEAD, RWKV_HEAD), F32)).astype(BF16)
    row = pl.BlockSpec((tm, D), lambda i: (i, 0))
    const = lambda shp: pl.BlockSpec(shp, lambda i: (0,) * len(shp))
    return pl.pallas_call(
        _rwkv_out_kernel,
        grid=(T // tm,),
        in_specs=[row, row, row, row, const((1, D)), const((1, D)), const((MXU_DIM, MXU_DIM))],
        out_specs=row,
        out_shape=jax.ShapeDtypeStruct((T, D), BF16),
        compiler_params=_cparams(("parallel",)),
        name="l1_rwkv_norm_gate",
    )(yf, yb, bon, g, ln_g.reshape(1, D), ln_b.reshape(1, D), bd)


def _cast_tables_kernel(u_ref, v_ref, ub_ref, vb_ref):
    ub_ref[...] = u_ref[...].astype(BF16)
    vb_ref[...] = v_ref[...].astype(BF16)


def _cast_tables(u_all, v_all, layer):
    _, E, D = u_all.shape
    rows = 1024
    src = pl.BlockSpec((None, rows, D), lambda i: (layer, i, 0))
    dst = pl.BlockSpec((rows, D), lambda i: (i, 0))
    return pl.pallas_call(
        _cast_tables_kernel,
        grid=(E // rows,),
        in_specs=[src, src],
        out_specs=[dst, dst],
        out_shape=[jax.ShapeDtypeStruct((E, D), BF16)] * 2,
        compiler_params=_cparams(("parallel",)),
        name="peer_tables_bf16",
    )(u_all, v_all)


def _peer_layer(x2d, y_halves, w_o, mod_l, norm_g2, w_q, subkeys, u_all, v_all, layer, B, S):
    x1, ht, st = _post(x2d, y_halves, w_o, mod_l, norm_g2, w_q, subkeys, B, S)
    r1, g1, n0, f0 = _topk(st)
    u_tab, v_tab = _cast_tables(u_all, v_all, layer)
    return _peer_dense(x1, mod_l, ht, u_tab, v_tab, r1, g1, n0, f0, B, S)


def kernel(x, c, ada_w, ada_b, norm_g, attn_w_in, attn_qk_g, diff_lambda, diff_subln_g, attn_w_out, rel_bias,
           rwkv_mix, rwkv_w_rkv, rwkv_w0, rwkv_w1, rwkv_w2, rwkv_a0, rwkv_a1, rwkv_a2, rwkv_g1, rwkv_g2,
           rwkv_k_k, rwkv_k_a, rwkv_r_k, rwkv_ln_g, rwkv_ln_b, rwkv_w_o,
           peer_w_q, peer_subkeys, peer_u, peer_v):
    B, S, D = x.shape
    mod = _ada(c, ada_w, ada_b)
    x2d = x.reshape(B * S, D)
    for layer in range(DEPTH):
        j = layer // 2
        mod_l = mod[layer]
        if layer % 2 == 0:
            lambda_init = 0.8 - 0.6 * math.exp(-0.3 * layer)
            qa, ka, va, qb, kb, vb = _inproj(x2d, mod_l, norm_g[layer, 0], attn_w_in[j], attn_qk_g[j], B, S)
            oa = _attn_a(qa, ka, va, diff_lambda[j], diff_subln_g[j], rel_bias, lambda_init, B, S)
            ob = _attn_b(qb, kb, vb, B, S)
            y_halves = ((oa, 0), (ob, 0))
            w_o = attn_w_out[j]
        else:
            r, v, g, kn, lw, kz, az, bon = _rwkv_pre(
                x2d, mod_l, norm_g[layer, 0], rwkv_mix[j], rwkv_w_rkv[j], rwkv_w0[j], rwkv_w1[j], rwkv_w2[j],
                rwkv_a0[j], rwkv_a1[j], rwkv_a2[j], rwkv_g1[j], rwkv_g2[j], rwkv_k_k[j], rwkv_k_a[j], rwkv_r_k[j], B, S)
            yf, yb = _rwkv_scan(r, v, kn, lw, kz, az, B, S)
            yo = _rwkv_out(yf, yb, bon, g, rwkv_ln_g[j], rwkv_ln_b[j])
            y_halves = ((yo, 0), (yo, 1))
            w_o = rwkv_w_o[j]
        x2d = _peer_layer(x2d, y_halves, w_o, mod_l, norm_g[layer, 1], peer_w_q[layer], peer_subkeys[layer],
                          peer_u, peer_v, layer, B, S)
    return x2d.reshape(B, S, D)
```
